```python
import jax, jax.numpy as jnp
from jax import lax
import numpy as np

D_MODEL = 2048
BATCH = 2
SEQ = 8192
DEPTH = 4

PLE_DIM = 256
D_CONV = D_MODEL // 2
CONV_K = 3
D_REC = D_MODEL // 2
HG_HEAD_DIM = 128
HG_HEADS = D_REC // HG_HEAD_DIM
HG_VDIM = D_REC // HG_HEADS
CHUNK = 64
EPS = 1e-6
LB_FLOOR = 1e-30
IN_SPLITS = [D_CONV, D_CONV, D_CONV, D_CONV,
             D_REC, D_REC, D_REC, D_REC,
             D_MODEL, D_MODEL]
IN_COLS = sum(IN_SPLITS)

kernel_name = "hybrid_shortconv_hgrn2_gated_merge"


def rmsnorm(x, g):
    xf = x.astype(jnp.float32)
    y = xf * lax.rsqrt(jnp.mean(xf * xf, axis=-1, keepdims=True) + EPS)
    return (y * g.astype(jnp.float32)).astype(x.dtype)


def causal_dwconv3(u, w):
    up = jnp.pad(u, ((0, 0), (CONV_K - 1, 0), (0, 0)))
    s = u.shape[1]
    return up[:, 0:s] * w[0] + up[:, 1:s + 1] * w[1] + up[:, 2:s + 2] * w[2]


def hgrn2_chunked(q, k, v, log_f):
    bn, sn, h, kd = q.shape
    vd = v.shape[-1]
    n = sn // CHUNK

    def to_chunks(t):
        return t.reshape(bn, n, CHUNK, h, t.shape[-1]).transpose(1, 0, 3, 2, 4)

    qc, kc, vc, fc = to_chunks(q), to_chunks(k), to_chunks(v), to_chunks(log_f)
    causal = jnp.tril(jnp.ones((CHUNK, CHUNK), dtype=bool))[:, :, None]
    causal_f = causal.astype(jnp.float32)

    def step(state, inp):
        qn, kn, vn, fn = inp
        b = jnp.cumsum(fn, axis=2)
        inter = jnp.einsum('bhtk,bhkv->bhtv', qn * jnp.exp(b), state)
        diff = b[:, :, :, None, :] - b[:, :, None, :, :]
        decay = jnp.exp(jnp.where(causal, diff, 0.0)) * causal_f
        a = jnp.einsum('bhtk,bhtsk,bhsk->bhts', qn, decay, kn)
        o = inter + jnp.einsum('bhts,bhsv->bhtv', a, vn)
        b_last = b[:, :, -1:, :]
        state = (jnp.exp(b_last[:, :, 0, :])[..., None] * state
                 + jnp.einsum('bhsk,bhsv->bhkv', kn * jnp.exp(b_last - b), vn))
        return state, o

    s0 = jnp.zeros((bn, h, kd, vd), jnp.float32)
    _, o = lax.scan(step, s0, (qc, kc, vc, fc))
    return o.transpose(1, 0, 3, 2, 4).reshape(bn, sn, h, vd)


def setup_inputs(seed: int = 0) -> dict:
    key = jax.random.key(seed)
    ks = jax.random.split(key, 16)
    f32 = jnp.float32
    nrm = lambda k, shp: jax.random.normal(k, shp, f32)
    return {
        "x": nrm(ks[0], (BATCH, SEQ, D_MODEL)),
        "p": nrm(ks[1], (DEPTH, BATCH, SEQ, PLE_DIM)),
        "norm_mix_g": 1.0 + 0.02 * nrm(ks[2], (DEPTH, D_MODEL)),
        "w_in": nrm(ks[3], (DEPTH, D_MODEL, IN_COLS)) * D_MODEL ** -0.5,
        "conv_w": nrm(ks[4], (DEPTH, CONV_K, D_CONV)) * CONV_K ** -0.5,
        "lb_param": 0.1 * nrm(ks[5], (DEPTH, D_REC)),
        "hg_norm_g": 1.0 + 0.02 * nrm(ks[6], (DEPTH, D_REC)),
        "w_a_out": nrm(ks[7], (DEPTH, D_CONV, D_MODEL)) * D_CONV ** -0.5,
        "w_b_out": nrm(ks[8], (DEPTH, D_REC, D_MODEL)) * D_REC ** -0.5,
        "w_o": nrm(ks[9], (DEPTH, D_MODEL, D_MODEL)) * D_MODEL ** -0.5,
        "ple_norm_g": 1.0 + 0.02 * nrm(ks[10], (DEPTH, D_MODEL)),
        "w_ple_gate": nrm(ks[11], (DEPTH, D_MODEL, D_MODEL)) * D_MODEL ** -0.5,
        "w_ple_proj": nrm(ks[12], (DEPTH, PLE_DIM, D_MODEL)) * PLE_DIM ** -0.5,
        "final_norm_g": 1.0 + 0.02 * nrm(ks[13], (D_MODEL,)),
    }


def reference(x, p, norm_mix_g, w_in, conv_w, lb_param, hg_norm_g, w_a_out, w_b_out,
              w_o, ple_norm_g, w_ple_gate, w_ple_proj, final_norm_g):
    dt = x.dtype
    bn, sn, _ = x.shape
    lb_sm = jax.nn.softmax(lb_param.astype(jnp.float32), axis=0)
    lb_all = jnp.cumsum(lb_sm, axis=0) - lb_sm[0]
    split_idx = [int(s) for s in np.cumsum(IN_SPLITS)[:-1]]

    for i in range(DEPTH):
        h = rmsnorm(x, norm_mix_g[i])
        u = h @ w_in[i]
        c_g, b_g, xa, za, q, fl, iv, og, ga, gb = jnp.split(u, split_idx, axis=-1)

        ya = b_g * causal_dwconv3(c_g * xa, conv_w[i]) * jax.nn.silu(za)

        lb = jnp.clip(lb_all[i], 0.0, 1.0 - 1e-6)
        flf = fl.astype(jnp.float32)
        log_f = jnp.logaddexp(jnp.log(jnp.maximum(lb, LB_FLOOR)),
                              jnp.log1p(-lb) + jax.nn.log_sigmoid(flf))
        kk = (1.0 - lb) * jax.nn.sigmoid(-flf)
        hshape = (bn, sn, HG_HEADS, HG_HEAD_DIM)
        o = hgrn2_chunked(jax.nn.silu(q.astype(jnp.float32)).reshape(hshape),
                          kk.reshape(hshape),
                          iv.astype(jnp.float32).reshape(bn, sn, HG_HEADS, HG_VDIM),
                          log_f.reshape(hshape))
        o = o * lax.rsqrt(jnp.mean(o * o, axis=-1, keepdims=True) + EPS)
        o = o.reshape(bn, sn, D_REC) * hg_norm_g[i].astype(jnp.float32)
        yb = o.astype(dt) * jax.nn.silu(og)

        m = jax.nn.sigmoid(ga) * (ya @ w_a_out[i]) + jax.nn.sigmoid(gb) * (yb @ w_b_out[i])
        x = x + m @ w_o[i]

        gate = jax.nn.sigmoid(rmsnorm(x, ple_norm_g[i]) @ w_ple_gate[i])
        x = x + gate * (p[i] @ w_ple_proj[i])

    return rmsnorm(x, final_norm_g)
```

```python
import functools
import math

import numpy as np
import jax
import jax.numpy as jnp
from jax import lax
from jax.experimental import pallas as pl
from jax.experimental.pallas import tpu as pltpu

D_MODEL = 2048
D_CONV = 1024
D_REC = 1024
HG_HEADS = 8
HG_DIM = 128
PLE_DIM = 256
CONV_K = 3
EPS = 1e-6
LB_FLOOR = 1e-30

ROW_TILE = 512
COL_BLOCK = 256
CHUNK = 128
N_LEVELS = int(math.log2(CHUNK))
HALO_ROWS = 8
VMEM_LIMIT_BYTES = 56 * 1024 * 1024

F32 = jnp.float32
BF16 = jnp.bfloat16


def _dot(a, b):
    return jnp.dot(a, b, preferred_element_type=F32)


def _dot_nt(a, b):
    return lax.dot_general(a, b, (((1,), (1,)), ((), ())), preferred_element_type=F32)


def _dot_tn(a, b):
    return lax.dot_general(a, b, (((0,), (0,)), ((), ())), preferred_element_type=F32)


def _rmsnorm(x, g):
    ms = jnp.mean(x * x, axis=-1, keepdims=True)
    return x * lax.rsqrt(ms + EPS) * g


def _sigmoid(x):
    return jax.nn.sigmoid(x)


def _silu(x):
    return x * jax.nn.sigmoid(x)


def _layer_spec(layer, shape):
    nd = len(shape)
    return pl.BlockSpec((None,) + tuple(shape), lambda *_: (layer,) + (0,) * nd,
                        pipeline_mode=pl.Buffered(1))


def _params(n_axes):
    return pltpu.CompilerParams(dimension_semantics=("arbitrary",) * n_axes,
                                vmem_limit_bytes=VMEM_LIMIT_BYTES)


def _branch_a_kernel(x_ref, xh_ref, g_ref, w_ref, cw_ref, ya_ref, *, tiles_per_seq):
    i = pl.program_id(0)
    g = g_ref[...]
    h = _rmsnorm(x_ref[...], g).astype(BF16)
    hh = _rmsnorm(xh_ref[...], g).astype(BF16)
    seq_start = (i % tiles_per_seq) == 0
    tm = h.shape[0]
    row = lax.broadcasted_iota(jnp.int32, (tm, COL_BLOCK), 0)
    cb = COL_BLOCK
    for j in range(D_CONV // cb):
        w = w_ref[:, j * 4 * cb:(j + 1) * 4 * cb]
        u = _dot(h, w)
        uh = _dot(hh, w)
        v = u[:, 0:cb] * u[:, 2 * cb:3 * cb]
        vh = uh[:, 0:cb] * uh[:, 2 * cb:3 * cb]
        vh = jnp.where(seq_start, 0.0, vh)
        p1 = vh[HALO_ROWS - 1:HALO_ROWS, :]
        p2 = vh[HALO_ROWS - 2:HALO_ROWS - 1, :]
        v1 = jnp.where(row == 0, p1, pltpu.roll(v, 1, 0))
        v2 = jnp.where(row == 0, p2, jnp.where(row == 1, p1, pltpu.roll(v, 2, 0)))
        cw = cw_ref[:, j * cb:(j + 1) * cb]
        conv = v2 * cw[0:1, :] + v1 * cw[1:2, :] + v * cw[2:3, :]
        ya = u[:, cb:2 * cb] * conv * _silu(u[:, 3 * cb:4 * cb])
        ya_ref[:, j * cb:(j + 1) * cb] = ya.astype(ya_ref.dtype)


def _branch_a(x2, g, w_a, conv_w, layer, seq_len):
    m = x2.shape[0]
    tm = ROW_TILE
    halo_blocks_per_tile = tm // HALO_ROWS
    kern = functools.partial(_branch_a_kernel, tiles_per_seq=seq_len // tm)
    return pl.pallas_call(
        kern,
        grid=(m // tm,),
        in_specs=[
            pl.BlockSpec((tm, D_MODEL), lambda i: (i, 0)),
            pl.BlockSpec((HALO_ROWS, D_MODEL),
                         lambda i: (jnp.maximum(i * halo_blocks_per_tile - 1, 0), 0)),
            pl.BlockSpec((None, 1, D_MODEL), lambda i: (layer, 0, 0)),
            _layer_spec(layer, (D_MODEL, 4 * D_CONV)),
            pl.BlockSpec((None, CONV_K, D_CONV), lambda i: (layer, 0, 0)),
        ],
        out_specs=pl.BlockSpec((tm, D_CONV), lambda i: (i, 0)),
        out_shape=jax.ShapeDtypeStruct((m, D_CONV), BF16),
        compiler_params=_params(1),
        name="branch_a",
    )(x2, x2, g, w_a, conv_w)


def _decay_tables():
    c = CHUNK
    t = np.arange(c)[:, None]
    j = np.arange(c)[None, :]
    blocks = []
    for l in range(N_LEVELS):
        h = 1 << l
        lower = (t & h) != 0
        start = (t // h) * h
        blocks.append((lower & (j >= start) & (j <= t)) | (~lower & (j > t) & (j < start + h)))
    blocks.append(j <= t)
    blocks.append(j > t)
    gstack = np.concatenate(blocks, axis=0).astype(np.float32)
    xor = np.maximum(t ^ j, 1)
    level = np.where(j < t, np.floor(np.log2(xor)).astype(np.int32),
                     np.where(j == t, N_LEVELS, -1)).astype(np.int32)
    return gstack, level


def _branch_b_kernel(x_ref, g_ref, w_ref, lbp_ref, hg_ref, gst_ref, lvl_ref, yb_ref,
                     q_s, k_s, lf_s, v_s, og_s, state_s, *, layer):
    i = pl.program_id(1)

    @pl.when(i == 0)
    def _():
        state_s[...] = jnp.zeros_like(state_s)

    lbp = lbp_ref[...]
    rows = [lbp[r:r + 1, :] for r in range(lbp.shape[0])]
    mx = functools.reduce(jnp.maximum, rows)
    ex = [jnp.exp(r - mx) for r in rows]
    den = functools.reduce(lambda a, b: a + b, ex)
    lb = jnp.zeros_like(mx)
    for r in range(1, layer + 1):
        lb = lb + ex[r] / den
    lb = jnp.clip(lb, 0.0, 1.0 - 1e-6)
    lb_floor = jnp.maximum(lb, LB_FLOOR)
    one_m_lb = 1.0 - lb

    h = _rmsnorm(x_ref[...], g_ref[...]).astype(BF16)
    cb = COL_BLOCK
    for j in range(D_REC // cb):
        u = _dot(h, w_ref[:, j * 4 * cb:(j + 1) * 4 * cb])
        sl = slice(j * cb, (j + 1) * cb)
        fl = u[:, cb:2 * cb]
        q_s[:, sl] = _silu(u[:, 0:cb])
        lf_s[:, sl] = jnp.log(lb_floor[:, sl] + one_m_lb[:, sl] * _sigmoid(fl))
        k_s[:, sl] = one_m_lb[:, sl] * _sigmoid(-fl)
        v_s[:, sl] = u[:, 2 * cb:3 * cb]
        og_s[:, sl] = _silu(u[:, 3 * cb:4 * cb])

    gst = gst_ref[...]
    lvl = lvl_ref[...]
    hg = hg_ref[...]
    c = CHUNK
    n_chunks = x_ref.shape[0] // c

    def chunk_body(ci, carry):
        r0 = pl.multiple_of(ci * c, c)
        rs = pl.ds(r0, c)
        for j in range(D_REC // cb):
            sl = slice(j * cb, (j + 1) * cb)
            lf = lf_s[rs, sl]
            lf_hi = lf.astype(BF16)
            lf_lo = (lf - lf_hi.astype(F32)).astype(BF16)
            e_all = jnp.exp(_dot(gst, lf_hi) + _dot(gst, lf_lo))
            for e in range(cb // HG_DIM):
                head = j * (cb // HG_DIM) + e
                hs = slice(j * cb + e * HG_DIM, j * cb + (e + 1) * HG_DIM)
                es = slice(e * HG_DIM, (e + 1) * HG_DIM)
                q = q_s[rs, hs]
                k = k_s[rs, hs]
                v = v_s[rs, hs].astype(BF16)
                a = jnp.where(lvl == N_LEVELS, _dot_nt(q.astype(BF16), k.astype(BF16)), 0.0)
                for l in range(N_LEVELS):
                    el = e_all[l * c:(l + 1) * c, es]
                    p = _dot_nt((q * el).astype(BF16), (k * el).astype(BF16))
                    a = jnp.where(lvl == l, p, a)
                e_pre = e_all[N_LEVELS * c:(N_LEVELS + 1) * c, es]
                e_suf = e_all[(N_LEVELS + 1) * c:(N_LEVELS + 2) * c, es]
                st = state_s[head]
                o = _dot_nt((q * e_pre).astype(BF16), st.astype(BF16)) + _dot(a.astype(BF16), v)
                state_s[head] = st * e_pre[c - 1:c, :] + _dot_tn(v, (k * e_suf).astype(BF16))
                o = o * lax.rsqrt(jnp.mean(o * o, axis=-1, keepdims=True) + EPS) * hg[:, hs]
                yb_ref[rs, hs] = (o * og_s[rs, hs]).astype(yb_ref.dtype)
        return carry

    lax.fori_loop(0, n_chunks, chunk_body, 0)


def _branch_b(x2, g, w_b, lb_param, hg_g, layer, batch, seq_len):
    m = x2.shape[0]
    tm = ROW_TILE
    tiles = seq_len // tm
    gstack, level = _decay_tables()
    gstack = jnp.asarray(gstack, BF16)
    level = jnp.asarray(level, jnp.int32)
    depth = lb_param.shape[0]
    kern = functools.partial(_branch_b_kernel, layer=layer)
    row_map = lambda b, i: (b * tiles + i, 0)
    const2 = lambda b, i: (0, 0)
    return pl.pallas_call(
        kern,
        grid=(batch, tiles),
        in_specs=[
            pl.BlockSpec((tm, D_MODEL), row_map),
            pl.BlockSpec((None, 1, D_MODEL), lambda b, i: (layer, 0, 0)),
            _layer_spec(layer, (D_MODEL, 4 * D_REC)),
            pl.BlockSpec((depth, D_REC), const2),
            pl.BlockSpec((None, 1, D_REC), lambda b, i: (layer, 0, 0)),
            pl.BlockSpec(gstack.shape, const2),
            pl.BlockSpec(level.shape, const2),
        ],
        out_specs=pl.BlockSpec((tm, D_REC), row_map),
        out_shape=jax.ShapeDtypeStruct((m, D_REC), BF16),
        scratch_shapes=[pltpu.VMEM((tm, D_REC), F32) for _ in range(5)]
        + [pltpu.VMEM((HG_HEADS, HG_DIM, HG_DIM), F32)],
        compiler_params=_params(2),
        name="branch_b",
    )(x2, g, w_b, lb_param, hg_g, gstack, level)


MERGE_BLOCK = 512


def _merge_kernel(x_ref, g_ref, ya_ref, yb_ref, wg_ref, wa_ref, wb_ref, m_ref):
    h = _rmsnorm(x_ref[...], g_ref[...]).astype(BF16)
    ya = ya_ref[...]
    yb = yb_ref[...]
    nb = MERGE_BLOCK
    for n in range(D_MODEL // nb):
        gates = _dot(h, wg_ref[:, n * 2 * nb:(n + 1) * 2 * nb])
        sl = slice(n * nb, (n + 1) * nb)
        m = (_sigmoid(gates[:, 0:nb]) * _dot(ya, wa_ref[:, sl])
             + _sigmoid(gates[:, nb:2 * nb]) * _dot(yb, wb_ref[:, sl]))
        m_ref[:, sl] = m.astype(m_ref.dtype)


def _merge(x2, g, ya, yb, w_g, w_a_out, w_b_out, layer):
    m = x2.shape[0]
    tm = ROW_TILE
    return pl.pallas_call(
        _merge_kernel,
        grid=(m // tm,),
        in_specs=[
            pl.BlockSpec((tm, D_MODEL), lambda i: (i, 0)),
            pl.BlockSpec((None, 1, D_MODEL), lambda i: (layer, 0, 0)),
            pl.BlockSpec((tm, D_CONV), lambda i: (i, 0)),
            pl.BlockSpec((tm, D_REC), lambda i: (i, 0)),
            _layer_spec(layer, (D_MODEL, 2 * D_MODEL)),
            _layer_spec(layer, (D_CONV, D_MODEL)),
            _layer_spec(layer, (D_REC, D_MODEL)),
        ],
        out_specs=pl.BlockSpec((tm, D_MODEL), lambda i: (i, 0)),
        out_shape=jax.ShapeDtypeStruct((m, D_MODEL), BF16),
        compiler_params=_params(1),
        name="merge",
    )(x2, g, ya, yb, w_g, w_a_out, w_b_out)


def _out_ple_kernel(x_ref, m_ref, wo_ref, pg_ref, wpg_ref, p_ref, wpp_ref, fg_ref, o_ref,
                    *, final):
    o_ref[...] = x_ref[...] + _dot(m_ref[...], wo_ref[...])
    xn = _rmsnorm(o_ref[...], pg_ref[...]).astype(BF16)
    p = p_ref[...].astype(BF16)
    nb = MERGE_BLOCK
    for n in range(D_MODEL // nb):
        sl = slice(n * nb, (n + 1) * nb)
        gate = _sigmoid(_dot(xn, wpg_ref[:, sl]))
        o_ref[:, sl] = o_ref[:, sl] + gate * _dot(p, wpp_ref[:, sl])
    if final:
        o_ref[...] = _rmsnorm(o_ref[...], fg_ref[...])


def _out_ple(x2, m_act, w_o, ple_g, w_pg, p_l, w_pp, final_g, layer, final):
    m = x2.shape[0]
    tm = ROW_TILE
    kern = functools.partial(_out_ple_kernel, final=final)
    return pl.pallas_call(
        kern,
        grid=(m // tm,),
        in_specs=[
            pl.BlockSpec((tm, D_MODEL), lambda i: (i, 0)),
            pl.BlockSpec((tm, D_MODEL), lambda i: (i, 0)),
            _layer_spec(layer, (D_MODEL, D_MODEL)),
            pl.BlockSpec((None, 1, D_MODEL), lambda i: (layer, 0, 0)),
            _layer_spec(layer, (D_MODEL, D_MODEL)),
            pl.BlockSpec((None, tm, PLE_DIM), lambda i: (layer, i, 0)),
            _layer_spec(layer, (PLE_DIM, D_MODEL)),
            pl.BlockSpec((1, D_MODEL), lambda i: (0, 0)),
        ],
        out_specs=pl.BlockSpec((tm, D_MODEL), lambda i: (i, 0)),
        out_shape=jax.ShapeDtypeStruct((m, D_MODEL), F32),
        compiler_params=_params(1),
        name="out_ple",
    )(x2, m_act, w_o, ple_g, w_pg, p_l, w_pp, final_g)


def _group_columns(w, n_groups, block):
    depth, k, n = w.shape
    width = n // n_groups
    w = w.reshape(depth, k, n_groups, width // block, block)
    return w.transpose(0, 1, 3, 2, 4).reshape(depth, k, n)


def kernel(x, p, norm_mix_g, w_in, conv_w, lb_param, hg_norm_g, w_a_out, w_b_out, w_o,
           ple_norm_g, w_ple_gate, w_ple_proj, final_norm_g):
    batch, seq_len, d = x.shape
    depth = w_in.shape[0]
    m = batch * seq_len
    assert d == D_MODEL and seq_len % ROW_TILE == 0 and ROW_TILE % CHUNK == 0

    w_in_bf = w_in.astype(BF16)
    w_a = _group_columns(w_in_bf[:, :, :4 * D_CONV], 4, COL_BLOCK)
    w_b = _group_columns(w_in_bf[:, :, 4 * D_CONV:4 * D_CONV + 4 * D_REC], 4, COL_BLOCK)
    w_g = _group_columns(w_in_bf[:, :, 4 * D_CONV + 4 * D_REC:], 2, MERGE_BLOCK)
    w_a_out_bf = w_a_out.astype(BF16)
    w_b_out_bf = w_b_out.astype(BF16)
    w_o_bf = w_o.astype(BF16)
    w_pg_bf = w_ple_gate.astype(BF16)
    w_pp_bf = w_ple_proj.astype(BF16)
    mix_g = norm_mix_g.reshape(depth, 1, D_MODEL)
    ple_g = ple_norm_g.reshape(depth, 1, D_MODEL)
    hg_g = hg_norm_g.reshape(depth, 1, D_REC)
    final_g = final_norm_g.reshape(1, D_MODEL)
    p2 = p.reshape(depth, m, PLE_DIM)

    x2 = x.reshape(m, D_MODEL)
    for layer in range(depth):
        ya = _branch_a(x2, mix_g, w_a, conv_w, layer, seq_len)
        yb = _branch_b(x2, mix_g, w_b, lb_param, hg_g, layer, batch, seq_len)
        m_act = _merge(x2, mix_g, ya, yb, w_g, w_a_out_bf, w_b_out_bf, layer)
        x2 = _out_ple(x2, m_act, w_o_bf, ple_g, w_pg_bf, p2, w_pp_bf, final_g, layer,
                      final=(layer == depth - 1))
    return x2.reshape(batch, seq_len, D_MODEL)
```

```python
import functools
import math

import numpy as np
import jax
import jax.numpy as jnp
from jax import lax
from jax.experimental import pallas as pl
from jax.experimental.pallas import tpu as pltpu

D_MODEL = 2048
D_CONV = 1024
D_REC = 1024
HG_HEADS = 8
HG_DIM = 128
PLE_DIM = 256
CONV_K = 3
EPS = 1e-6
LB_FLOOR = 1e-30

ROW_TILE = 512
COL_BLOCK = 256
MERGE_BLOCK = 512
CHUNK = 128
N_LEVELS = int(math.log2(CHUNK))
SUBLANES = 8
VMEM_LIMIT_BYTES = 56 * 1024 * 1024

F32 = jnp.float32
BF16 = jnp.bfloat16


def _dot(a, b):
    return jnp.dot(a, b, preferred_element_type=F32)


def _dot_nt(a, b):
    return lax.dot_general(a, b, (((1,), (1,)), ((), ())), preferred_element_type=F32)


def _dot_tn(a, b):
    return lax.dot_general(a, b, (((0,), (0,)), ((), ())), preferred_element_type=F32)


def _rmsnorm(x, g):
    ms = jnp.mean(x * x, axis=-1, keepdims=True)
    return x * lax.rsqrt(ms + EPS) * g


def _sigmoid(x):
    return jax.nn.sigmoid(x)


def _silu(x):
    return x * jax.nn.sigmoid(x)


def _layer_spec(layer, shape, col_block=0):
    nd = len(shape)
    idx = (layer,) + (0,) * (nd - 1) + (col_block,)
    return pl.BlockSpec((None,) + tuple(shape), lambda *_: idx, pipeline_mode=pl.Buffered(1))


def _params(n_axes):
    return pltpu.CompilerParams(dimension_semantics=("arbitrary",) * n_axes,
                                vmem_limit_bytes=VMEM_LIMIT_BYTES)


def _branch_a_kernel(x_ref, xh_ref, g_ref, w_ref, cw_ref, ya_ref, *, tiles_per_seq):
    i = pl.program_id(0)
    g = g_ref[...]
    h = _rmsnorm(x_ref[...], g).astype(BF16)
    hh = _rmsnorm(xh_ref[...], g).astype(BF16)
    seq_start = (i % tiles_per_seq) == 0
    tm = h.shape[0]
    cb = COL_BLOCK
    row = lax.broadcasted_iota(jnp.int32, (tm, cb), 0)
    for j in range(D_CONV // cb):
        w_c, w_b, w_x, w_z = [w_ref[:, grp * D_CONV + j * cb:grp * D_CONV + (j + 1) * cb]
                              for grp in range(4)]
        v = _dot(h, w_c) * _dot(h, w_x)
        vh = _dot(hh, w_c) * _dot(hh, w_x)
        vh = jnp.where(seq_start, 0.0, vh)
        p1 = vh[SUBLANES - 1:SUBLANES, :]
        p2 = vh[SUBLANES - 2:SUBLANES - 1, :]
        v1 = jnp.where(row == 0, p1, pltpu.roll(v, 1, 0))
        v2 = jnp.where(row == 0, p2, jnp.where(row == 1, p1, pltpu.roll(v, 2, 0)))
        cw = cw_ref[:, j * cb:(j + 1) * cb]
        conv = v2 * cw[0:1, :] + v1 * cw[1:2, :] + v * cw[2:3, :]
        ya = _dot(h, w_b) * conv * _silu(_dot(h, w_z))
        ya_ref[:, j * cb:(j + 1) * cb] = ya.astype(ya_ref.dtype)


def _branch_a(x2, g, w_in_bf, conv_w, layer, seq_len):
    m = x2.shape[0]
    tm = ROW_TILE
    halo_blocks_per_tile = tm // SUBLANES
    kern = functools.partial(_branch_a_kernel, tiles_per_seq=seq_len // tm)
    return pl.pallas_call(
        kern,
        grid=(m // tm,),
        in_specs=[
            pl.BlockSpec((tm, D_MODEL), lambda i: (i, 0)),
            pl.BlockSpec((SUBLANES, D_MODEL),
                         lambda i: (jnp.maximum(i * halo_blocks_per_tile - 1, 0), 0)),
            pl.BlockSpec((None, 1, D_MODEL), lambda i: (layer, 0, 0)),
            _layer_spec(layer, (D_MODEL, 4 * D_CONV), col_block=0),
            pl.BlockSpec((None, CONV_K, D_CONV), lambda i: (layer, 0, 0)),
        ],
        out_specs=pl.BlockSpec((tm, D_CONV), lambda i: (i, 0)),
        out_shape=jax.ShapeDtypeStruct((m, D_CONV), BF16),
        compiler_params=_params(1),
        name="branch_a",
    )(x2, x2, g, w_in_bf, conv_w)


def _level_table():
    c = CHUNK
    t = np.arange(c)[:, None]
    j = np.arange(c)[None, :]
    xor = np.maximum(t ^ j, 1)
    return np.where(j < t, np.floor(np.log2(xor)).astype(np.int32),
                    np.where(j == t, N_LEVELS, -1)).astype(np.int32)


def _bcast_row(ref, r, cols, n):
    return jnp.broadcast_to(ref[r:r + 1, cols], (n, HG_DIM))


def _level_operand(l, q, k, b, lf, b_ref, cols, row):
    c = CHUNK
    h = 1 << l
    blk = 2 * h
    if l == 0:
        lower = (row & 1) != 0
        x = jnp.where(lower, lf, 0.0)
        m = jnp.where(lower, q, k)
    elif blk <= SUBLANES:
        lower = (row & h) != 0
        ref = None
        for sb in range(SUBLANES // blk):
            part = jnp.concatenate(
                [_bcast_row(b_ref, v0 + sb * blk + h - 1, cols, SUBLANES)
                 for v0 in range(0, c, SUBLANES)], axis=0)
            ref = part if ref is None else jnp.where((row & (SUBLANES - 1)) < sb * blk, ref, part)
        x = jnp.where(lower, b - ref, ref - b)
        m = jnp.where(lower, q, k)
    else:
        xs, ms = [], []
        for b0 in range(0, c, blk):
            mid = _bcast_row(b_ref, b0 + h - 1, cols, h)
            xs += [mid - b[b0:b0 + h], b[b0 + h:b0 + blk] - mid]
            ms += [k[b0:b0 + h], q[b0 + h:b0 + blk]]
        x = jnp.concatenate(xs, axis=0)
        m = jnp.concatenate(ms, axis=0)
    return (m * jnp.exp2(x)).astype(BF16)


def _branch_b_kernel(x_ref, g_ref, w_ref, lbp_ref, hg_ref, tril_ref, lvl_ref, yb_ref,
                     q_s, k_s, lf_s, v_s, og_s, b_s, state_s, *, layer):
    i = pl.program_id(1)

    @pl.when(i == 0)
    def _():
        state_s[...] = jnp.zeros_like(state_s)

    lbp = lbp_ref[...]
    rows = [lbp[r:r + 1, :] for r in range(lbp.shape[0])]
    mx = functools.reduce(jnp.maximum, rows)
    ex = [jnp.exp(r - mx) for r in rows]
    den = functools.reduce(lambda a, b: a + b, ex)
    lb = jnp.zeros_like(mx)
    for r in range(1, layer + 1):
        lb = lb + ex[r] / den
    lb = jnp.clip(lb, 0.0, 1.0 - 1e-6)
    lb_floor = jnp.maximum(lb, LB_FLOOR)
    one_m_lb = 1.0 - lb

    h = _rmsnorm(x_ref[...], g_ref[...]).astype(BF16)
    cb = COL_BLOCK
    for j in range(D_REC // cb):
        sl = slice(j * cb, (j + 1) * cb)
        u_q, u_f, u_i, u_g = [_dot(h, w_ref[:, grp * D_REC + j * cb:grp * D_REC + (j + 1) * cb])
                              for grp in range(4)]
        q_s[:, sl] = _silu(u_q)
        lf_s[:, sl] = jnp.log2(lb_floor[:, sl] + one_m_lb[:, sl] * _sigmoid(u_f))
        k_s[:, sl] = one_m_lb[:, sl] * _sigmoid(-u_f)
        v_s[:, sl] = u_i
        og_s[:, sl] = _silu(u_g)

    tril = tril_ref[...]
    lvl = lvl_ref[...]
    hg = hg_ref[...]
    c = CHUNK
    n_chunks = x_ref.shape[0] // c
    row = lax.broadcasted_iota(jnp.int32, (c, HG_DIM), 0)

    def chunk_body(ci, carry):
        r0 = pl.multiple_of(ci * c, c)
        rs = pl.ds(r0, c)
        lf_all = lf_s[rs, :]
        hi = lf_all.astype(BF16)
        rem = lf_all - hi.astype(F32)
        mid = rem.astype(BF16)
        lo = (rem - mid.astype(F32)).astype(BF16)
        b_s[...] = _dot(tril, hi) + _dot(tril, mid) + _dot(tril, lo)
        for head in range(HG_HEADS):
            hs = slice(head * HG_DIM, (head + 1) * HG_DIM)
            q = q_s[rs, hs]
            k = k_s[rs, hs]
            lf = lf_s[rs, hs]
            b = b_s[:, hs]
            v = v_s[rs, hs].astype(BF16)
            a = jnp.where(lvl == N_LEVELS, _dot_nt(q.astype(BF16), k.astype(BF16)), 0.0)
            for l in range(N_LEVELS):
                op = _level_operand(l, q, k, b, lf, b_s, hs, row)
                a = jnp.where(lvl == l, _dot_nt(op, op), a)
            b_last = _bcast_row(b_s, c - 1, hs, c)
            st = state_s[head]
            o = (_dot_nt((q * jnp.exp2(b)).astype(BF16), st.astype(BF16))
                 + _dot(a.astype(BF16), v))
            state_s[head] = (st * jnp.exp2(b_last[0:1, :])
                             + _dot_tn(v, (k * jnp.exp2(b_last - b)).astype(BF16)))
            o = o * lax.rsqrt(jnp.mean(o * o, axis=-1, keepdims=True) + EPS) * hg[:, hs]
            yb_ref[rs, hs] = (o * og_s[rs, hs]).astype(yb_ref.dtype)
        return carry

    lax.fori_loop(0, n_chunks, chunk_body, 0)


def _branch_b(x2, g, w_in_bf, lb_param, hg_g, layer, batch, seq_len):
    m = x2.shape[0]
    tm = ROW_TILE
    tiles = seq_len // tm
    c = CHUNK
    tril = jnp.asarray(np.tril(np.ones((c, c), np.float32)), BF16)
    level = jnp.asarray(_level_table(), jnp.int32)
    depth = lb_param.shape[0]
    kern = functools.partial(_branch_b_kernel, layer=layer)
    row_map = lambda b, i: (b * tiles + i, 0)
    const2 = lambda b, i: (0, 0)
    return pl.pallas_call(
        kern,
        grid=(batch, tiles),
        in_specs=[
            pl.BlockSpec((tm, D_MODEL), row_map),
            pl.BlockSpec((None, 1, D_MODEL), lambda b, i: (layer, 0, 0)),
            _layer_spec(layer, (D_MODEL, 4 * D_REC), col_block=1),
            pl.BlockSpec((depth, D_REC), const2),
            pl.BlockSpec((None, 1, D_REC), lambda b, i: (layer, 0, 0)),
            pl.BlockSpec(tril.shape, const2),
            pl.BlockSpec(level.shape, const2),
        ],
        out_specs=pl.BlockSpec((tm, D_REC), row_map),
        out_shape=jax.ShapeDtypeStruct((m, D_REC), BF16),
        scratch_shapes=[pltpu.VMEM((tm, D_REC), F32) for _ in range(5)]
        + [pltpu.VMEM((c, D_REC), F32), pltpu.VMEM((HG_HEADS, HG_DIM, HG_DIM), F32)],
        compiler_params=_params(2),
        name="branch_b",
    )(x2, g, w_in_bf, lb_param, hg_g, tril, level)


def _merge_kernel(x_ref, g_ref, ya_ref, yb_ref, wg_ref, wa_ref, wb_ref, m_ref):
    h = _rmsnorm(x_ref[...], g_ref[...]).astype(BF16)
    ya = ya_ref[...]
    yb = yb_ref[...]
    nb = MERGE_BLOCK
    for n in range(D_MODEL // nb):
        sl = slice(n * nb, (n + 1) * nb)
        gate_a = _dot(h, wg_ref[:, n * nb:(n + 1) * nb])
        gate_b = _dot(h, wg_ref[:, D_MODEL + n * nb:D_MODEL + (n + 1) * nb])
        m = (_sigmoid(gate_a) * _dot(ya, wa_ref[:, sl])
             + _sigmoid(gate_b) * _dot(yb, wb_ref[:, sl]))
        m_ref[:, sl] = m.astype(m_ref.dtype)


def _merge(x2, g, ya, yb, w_in_bf, w_a_out, w_b_out, layer):
    m = x2.shape[0]
    tm = ROW_TILE
    return pl.pallas_call(
        _merge_kernel,
        grid=(m // tm,),
        in_specs=[
            pl.BlockSpec((tm, D_MODEL), lambda i: (i, 0)),
            pl.BlockSpec((None, 1, D_MODEL), lambda i: (layer, 0, 0)),
            pl.BlockSpec((tm, D_CONV), lambda i: (i, 0)),
            pl.BlockSpec((tm, D_REC), lambda i: (i, 0)),
            _layer_spec(layer, (D_MODEL, 2 * D_MODEL), col_block=2),
            _layer_spec(layer, (D_CONV, D_MODEL)),
            _layer_spec(layer, (D_REC, D_MODEL)),
        ],
        out_specs=pl.BlockSpec((tm, D_MODEL), lambda i: (i, 0)),
        out_shape=jax.ShapeDtypeStruct((m, D_MODEL), BF16),
        compiler_params=_params(1),
        name="merge",
    )(x2, g, ya, yb, w_in_bf, w_a_out, w_b_out)


def _out_ple_kernel(x_ref, m_ref, wo_ref, pg_ref, wpg_ref, p_ref, wpp_ref, fg_ref, o_ref,
                    *, final):
    o_ref[...] = x_ref[...] + _dot(m_ref[...], wo_ref[...])
    xn = _rmsnorm(o_ref[...], pg_ref[...]).astype(BF16)
    p = p_ref[...].astype(BF16)
    nb = MERGE_BLOCK
    for n in range(D_MODEL // nb):
        sl = slice(n * nb, (n + 1) * nb)
        gate = _sigmoid(_dot(xn, wpg_ref[:, sl]))
        o_ref[:, sl] = o_ref[:, sl] + gate * _dot(p, wpp_ref[:, sl])
    if final:
        o_ref[...] = _rmsnorm(o_ref[...], fg_ref[...])


def _out_ple(x2, m_act, w_o, ple_g, w_pg, p_l, w_pp, final_g, layer, final):
    m = x2.shape[0]
    tm = ROW_TILE
    kern = functools.partial(_out_ple_kernel, final=final)
    return pl.pallas_call(
        kern,
        grid=(m // tm,),
        in_specs=[
            pl.BlockSpec((tm, D_MODEL), lambda i: (i, 0)),
            pl.BlockSpec((tm, D_MODEL), lambda i: (i, 0)),
            _layer_spec(layer, (D_MODEL, D_MODEL)),
            pl.BlockSpec((None, 1, D_MODEL), lambda i: (layer, 0, 0)),
            _layer_spec(layer, (D_MODEL, D_MODEL)),
            pl.BlockSpec((None, tm, PLE_DIM), lambda i: (layer, i, 0)),
            _layer_spec(layer, (PLE_DIM, D_MODEL)),
            pl.BlockSpec((1, D_MODEL), lambda i: (0, 0)),
        ],
        out_specs=pl.BlockSpec((tm, D_MODEL), lambda i: (i, 0)),
        out_shape=jax.ShapeDtypeStruct((m, D_MODEL), F32),
        compiler_params=_params(1),
        name="out_ple",
    )(x2, m_act, w_o, ple_g, w_pg, p_l, w_pp, final_g)


def kernel(x, p, norm_mix_g, w_in, conv_w, lb_param, hg_norm_g, w_a_out, w_b_out, w_o,
           ple_norm_g, w_ple_gate, w_ple_proj, final_norm_g):
    batch, seq_len, d = x.shape
    depth = w_in.shape[0]
    m = batch * seq_len
    assert d == D_MODEL and seq_len % ROW_TILE == 0 and ROW_TILE % CHUNK == 0
    assert w_in.shape[2] == 4 * D_CONV + 4 * D_REC + 2 * D_MODEL and 4 * D_CONV == 4 * D_REC

    w_in_bf = w_in.astype(BF16)
    w_a_out_bf = w_a_out.astype(BF16)
    w_b_out_bf = w_b_out.astype(BF16)
    w_o_bf = w_o.astype(BF16)
    w_pg_bf = w_ple_gate.astype(BF16)
    w_pp_bf = w_ple_proj.astype(BF16)
    mix_g = norm_mix_g.reshape(depth, 1, D_MODEL)
    ple_g = ple_norm_g.reshape(depth, 1, D_MODEL)
    hg_g = hg_norm_g.reshape(depth, 1, D_REC)
    final_g = final_norm_g.reshape(1, D_MODEL)
    p2 = p.reshape(depth, m, PLE_DIM)

    x2 = x.reshape(m, D_MODEL)
    for layer in range(depth):
        ya = _branch_a(x2, mix_g, w_in_bf, conv_w, layer, seq_len)
        yb = _branch_b(x2, mix_g, w_in_bf, lb_param, hg_g, layer, batch, seq_len)
        m_act = _merge(x2, mix_g, ya, yb, w_in_bf, w_a_out_bf, w_b_out_bf, layer)
        x2 = _out_ple(x2, m_act, w_o_bf, ple_g, w_pg_bf, p2, w_pp_bf, final_g, layer,
                      final=(layer == depth - 1))
    return x2.reshape(batch, seq_len, D_MODEL)
```

```python
import functools
import math

import numpy as np
import jax
import jax.numpy as jnp
from jax import lax
from jax.experimental import pallas as pl
from jax.experimental.pallas import tpu as pltpu

D_MODEL = 2048
D_CONV = 1024
D_REC = 1024
HG_HEADS = 8
HG_DIM = 128
PLE_DIM = 256
CONV_K = 3
EPS = 1e-6
LB_FLOOR = 1e-30

ROW_TILE = 512
MIX_TILE = 256
COL_BLOCK = 256
MERGE_BLOCK = 512
CHUNK = 128
N_LEVELS = int(math.log2(CHUNK))
SUBLANES = 8
VMEM_LIMIT_BYTES = 56 * 1024 * 1024

F32 = jnp.float32
BF16 = jnp.bfloat16


def _dot(a, b):
    return jnp.dot(a, b, preferred_element_type=F32)


def _dot_nt(a, b):
    return lax.dot_general(a, b, (((1,), (1,)), ((), ())), preferred_element_type=F32)


def _dot_tn(a, b):
    return lax.dot_general(a, b, (((0,), (0,)), ((), ())), preferred_element_type=F32)


def _rmsnorm(x, g):
    ms = jnp.mean(x * x, axis=-1, keepdims=True)
    return x * lax.rsqrt(ms + EPS) * g


def _sigmoid(x):
    return jax.nn.sigmoid(x)


def _silu(x):
    return x * jax.nn.sigmoid(x)


def _layer_spec(layer, shape, col_block=0):
    nd = len(shape)
    idx = (layer,) + (0,) * (nd - 1) + (col_block,)
    return pl.BlockSpec((None,) + tuple(shape), lambda *_: idx, pipeline_mode=pl.Buffered(1))


def _params(n_axes):
    return pltpu.CompilerParams(dimension_semantics=("arbitrary",) * n_axes,
                                vmem_limit_bytes=VMEM_LIMIT_BYTES)


def _level_table():
    c = CHUNK
    t = np.arange(c)[:, None]
    j = np.arange(c)[None, :]
    xor = np.maximum(t ^ j, 1)
    return np.where(j < t, np.floor(np.log2(xor)).astype(np.int32),
                    np.where(j == t, N_LEVELS, -1)).astype(np.int32)


def _bcast_row(ref, r, n):
    return jnp.broadcast_to(ref[r:r + 1, :], (n, HG_DIM))


def _level_operand(l, q, k, b, lf, b_ref, row):
    c = CHUNK
    h = 1 << l
    blk = 2 * h
    if l == 0:
        lower = (row & 1) != 0
        x = jnp.where(lower, lf, 0.0)
        m = jnp.where(lower, q, k)
    elif blk <= SUBLANES:
        lower = (row & h) != 0
        ref = None
        for sb in range(SUBLANES // blk):
            part = jnp.concatenate(
                [_bcast_row(b_ref, v0 + sb * blk + h - 1, SUBLANES)
                 for v0 in range(0, c, SUBLANES)], axis=0)
            ref = part if ref is None else jnp.where((row & (SUBLANES - 1)) < sb * blk, ref, part)
        x = jnp.where(lower, b - ref, ref - b)
        m = jnp.where(lower, q, k)
    else:
        xs, ms = [], []
        for b0 in range(0, c, blk):
            mid = _bcast_row(b_ref, b0 + h - 1, h)
            xs += [mid - b[b0:b0 + h], b[b0 + h:b0 + blk] - mid]
            ms += [k[b0:b0 + h], q[b0 + h:b0 + blk]]
        x = jnp.concatenate(xs, axis=0)
        m = jnp.concatenate(ms, axis=0)
    return (m * jnp.exp2(x)).astype(BF16)


OP_INTER = N_LEVELS
OP_STATE = N_LEVELS + 1
OP_DIAG = N_LEVELS + 2
N_OPERANDS = N_LEVELS + 3


def _mixers_kernel(x_ref, xh_ref, g_ref, wa_ref, wb_ref, cw_ref, lbp_ref, hg_ref, tril_ref,
                   lvl_ref, ya_ref, yb_ref,
                   q_s, k_s, lf_s, b_s, v_s, og_s, bc_s, op_s, dec_s, state_s,
                   *, layer, tiles_per_seq):
    i = pl.program_id(0)
    seq_start = (i % tiles_per_seq) == 0

    @pl.when(seq_start)
    def _():
        state_s[...] = jnp.zeros_like(state_s)

    lbp = lbp_ref[...]
    rows = [lbp[r:r + 1, :] for r in range(lbp.shape[0])]
    mx = functools.reduce(jnp.maximum, rows)
    ex = [jnp.exp(r - mx) for r in rows]
    den = functools.reduce(lambda a, b: a + b, ex)
    lb = jnp.zeros_like(mx)
    for r in range(1, layer + 1):
        lb = lb + ex[r] / den
    lb = jnp.clip(lb, 0.0, 1.0 - 1e-6)
    lb_floor = jnp.maximum(lb, LB_FLOOR)
    one_m_lb = 1.0 - lb

    c = CHUNK
    tm = x_ref.shape[0]
    n_chunks = tm // c
    g = g_ref[...]
    tril = tril_ref[...]
    h = _rmsnorm(x_ref[...], g).astype(BF16)
    hh = _rmsnorm(xh_ref[...], g).astype(BF16)
    cb = COL_BLOCK
    heads_per_block = cb // HG_DIM

    for j in range(D_REC // cb):
        sl = slice(j * cb, (j + 1) * cb)
        u_q, u_f, u_i, u_g = [_dot(h, wb_ref[:, grp * D_REC + j * cb:grp * D_REC + (j + 1) * cb])
                              for grp in range(4)]
        qv = _silu(u_q)
        lfv = jnp.log2(lb_floor[:, sl] + one_m_lb[:, sl] * _sigmoid(u_f))
        kv = one_m_lb[:, sl] * _sigmoid(-u_f)
        ogv = _silu(u_g)
        for e in range(heads_per_block):
            head = j * heads_per_block + e
            es = slice(e * HG_DIM, (e + 1) * HG_DIM)
            q_s[head] = qv[:, es]
            k_s[head] = kv[:, es]
            lf_s[head] = lfv[:, es]
            v_s[head] = u_i[:, es].astype(BF16)
            og_s[head] = ogv[:, es]
        for ci in range(n_chunks):
            lfc = lfv[ci * c:(ci + 1) * c]
            hi = lfc.astype(BF16)
            rem = lfc - hi.astype(F32)
            mid = rem.astype(BF16)
            lo = (rem - mid.astype(F32)).astype(BF16)
            pre = _dot(tril, hi) + _dot(tril, mid) + _dot(tril, lo)
            for e in range(heads_per_block):
                b_s[j * heads_per_block + e, ci * c:(ci + 1) * c, :] = (
                    pre[:, e * HG_DIM:(e + 1) * HG_DIM])

    lvl = lvl_ref[...]
    row = lax.broadcasted_iota(jnp.int32, (c, HG_DIM), 0)
    row_a = lax.broadcasted_iota(jnp.int32, (tm, cb), 0)

    def prepare(ci, head):
        rs = slice(ci * c, (ci + 1) * c)
        q = q_s[head, rs, :]
        k = k_s[head, rs, :]
        lf = lf_s[head, rs, :]
        b = b_s[head, rs, :]
        bc = bc_s.at[head]
        bc[...] = b
        for l in range(N_LEVELS):
            op_s[head, l] = _level_operand(l, q, k, b, lf, bc, row)
        b_last = _bcast_row(bc, c - 1, c)
        op_s[head, OP_INTER] = (q * jnp.exp2(b)).astype(BF16)
        op_s[head, OP_STATE] = (k * jnp.exp2(b_last - b)).astype(BF16)
        diag = jnp.sum(q * k, axis=-1, keepdims=True)
        op_s[head, OP_DIAG] = jnp.broadcast_to(diag, (c, HG_DIM)).astype(BF16)
        dec_s[head] = jnp.exp2(b_last[0:SUBLANES, :])

    def contract_chunk(ci, filler_a, filler_b):
        rs = slice(ci * c, (ci + 1) * c)
        heads = range(HG_HEADS)
        states = [state_s[hd] for hd in heads]
        att = [jnp.where(lvl == N_LEVELS, op_s[hd, OP_DIAG], jnp.zeros((c, c), BF16))
               for hd in heads]
        for l in range(N_LEVELS):
            for hd in heads:
                op = op_s[hd, l]
                att[hd] = jnp.where(lvl == l, _dot_nt(op, op).astype(BF16), att[hd])
        filler_a()
        inter = [_dot_nt(op_s[hd, OP_INTER], states[hd].astype(BF16)) for hd in heads]
        vals = [v_s[hd, rs, :] for hd in heads]
        for hd in heads:
            state_s[hd] = states[hd] * dec_s[hd][0:1, :] + _dot_tn(vals[hd], op_s[hd, OP_STATE])
        outs = [inter[hd] + _dot(att[hd], vals[hd]) for hd in heads]
        filler_b()
        for hd in heads:
            o = outs[hd]
            o = o * lax.rsqrt(jnp.mean(o * o, axis=-1, keepdims=True) + EPS) * hg_ref[hd]
            yb_ref[rs, hd * HG_DIM:(hd + 1) * HG_DIM] = (o * og_s[hd, rs, :]).astype(yb_ref.dtype)

    def conv_block(j):
        w_c, w_b, w_x, w_z = [wa_ref[:, grp * D_CONV + j * cb:grp * D_CONV + (j + 1) * cb]
                              for grp in range(4)]
        v = _dot(h, w_c) * _dot(h, w_x)
        vh = _dot(hh, w_c) * _dot(hh, w_x)
        vh = jnp.where(seq_start, 0.0, vh)
        p1 = vh[SUBLANES - 1:SUBLANES, :]
        p2 = vh[SUBLANES - 2:SUBLANES - 1, :]
        v1 = jnp.where(row_a == 0, p1, pltpu.roll(v, 1, 0))
        v2 = jnp.where(row_a == 0, p2, jnp.where(row_a == 1, p1, pltpu.roll(v, 2, 0)))
        cw = cw_ref[:, j * cb:(j + 1) * cb]
        conv = v2 * cw[0:1, :] + v1 * cw[1:2, :] + v * cw[2:3, :]
        ya = _dot(h, w_b) * conv * _silu(_dot(h, w_z))
        ya_ref[:, j * cb:(j + 1) * cb] = ya.astype(ya_ref.dtype)

    n_stages = 2 * n_chunks

    def conv_blocks(stage):
        def run():
            for j in range(stage, D_CONV // cb, n_stages):
                conv_block(j)
        return run

    for hd in range(HG_HEADS):
        prepare(0, hd)
    for ci in range(n_chunks):
        contract_chunk(ci, conv_blocks(2 * ci), conv_blocks(2 * ci + 1))
        if ci + 1 < n_chunks:
            for hd in range(HG_HEADS):
                prepare(ci + 1, hd)


def _mixers(x2, g, w_in_bf, conv_w, lb_param, hg_g, layer, seq_len):
    m = x2.shape[0]
    tm = MIX_TILE
    c = CHUNK
    tril = jnp.asarray(np.tril(np.ones((c, c), np.float32)), BF16)
    level = jnp.asarray(_level_table(), BF16)
    depth = lb_param.shape[0]
    hg_heads = hg_g.reshape(depth, HG_HEADS, 1, HG_DIM)
    halo_blocks_per_tile = tm // SUBLANES
    kern = functools.partial(_mixers_kernel, layer=layer, tiles_per_seq=seq_len // tm)
    const2 = lambda i: (0, 0)
    head_f32 = pltpu.VMEM((HG_HEADS, tm, HG_DIM), F32)
    head_bf16 = pltpu.VMEM((HG_HEADS, tm, HG_DIM), BF16)
    return pl.pallas_call(
        kern,
        grid=(m // tm,),
        in_specs=[
            pl.BlockSpec((tm, D_MODEL), lambda i: (i, 0)),
            pl.BlockSpec((SUBLANES, D_MODEL),
                         lambda i: (jnp.maximum(i * halo_blocks_per_tile - 1, 0), 0)),
            pl.BlockSpec((None, 1, D_MODEL), lambda i: (layer, 0, 0)),
            _layer_spec(layer, (D_MODEL, 4 * D_CONV), col_block=0),
            _layer_spec(layer, (D_MODEL, 4 * D_REC), col_block=1),
            pl.BlockSpec((None, CONV_K, D_CONV), lambda i: (layer, 0, 0)),
            pl.BlockSpec((depth, D_REC), const2),
            pl.BlockSpec((None, HG_HEADS, 1, HG_DIM), lambda i: (layer, 0, 0, 0)),
            pl.BlockSpec(tril.shape, const2),
            pl.BlockSpec(level.shape, const2),
        ],
        out_specs=[pl.BlockSpec((tm, D_CONV), lambda i: (i, 0)),
                   pl.BlockSpec((tm, D_REC), lambda i: (i, 0))],
        out_shape=[jax.ShapeDtypeStruct((m, D_CONV), BF16),
                   jax.ShapeDtypeStruct((m, D_REC), BF16)],
        scratch_shapes=[head_f32, head_f32, head_f32, head_f32, head_bf16, head_f32,
                        pltpu.VMEM((HG_HEADS, c, HG_DIM), F32),
                        pltpu.VMEM((HG_HEADS, N_OPERANDS, c, HG_DIM), BF16),
                        pltpu.VMEM((HG_HEADS, SUBLANES, HG_DIM), F32),
                        pltpu.VMEM((HG_HEADS, HG_DIM, HG_DIM), F32)],
        compiler_params=_params(1),
        name="mixers",
    )(x2, x2, g, w_in_bf, w_in_bf, conv_w, lb_param, hg_heads, tril, level)


def _merge_kernel(x_ref, g_ref, ya_ref, yb_ref, wg_ref, wa_ref, wb_ref, m_ref):
    h = _rmsnorm(x_ref[...], g_ref[...]).astype(BF16)
    ya = ya_ref[...]
    yb = yb_ref[...]
    nb = MERGE_BLOCK
    for n in range(D_MODEL // nb):
        sl = slice(n * nb, (n + 1) * nb)
        gate_a = _dot(h, wg_ref[:, n * nb:(n + 1) * nb])
        gate_b = _dot(h, wg_ref[:, D_MODEL + n * nb:D_MODEL + (n + 1) * nb])
        m = (_sigmoid(gate_a) * _dot(ya, wa_ref[:, sl])
             + _sigmoid(gate_b) * _dot(yb, wb_ref[:, sl]))
        m_ref[:, sl] = m.astype(m_ref.dtype)


def _merge(x2, g, ya, yb, w_in_bf, w_a_out, w_b_out, layer):
    m = x2.shape[0]
    tm = ROW_TILE
    return pl.pallas_call(
        _merge_kernel,
        grid=(m // tm,),
        in_specs=[
            pl.BlockSpec((tm, D_MODEL), lambda i: (i, 0)),
            pl.BlockSpec((None, 1, D_MODEL), lambda i: (layer, 0, 0)),
            pl.BlockSpec((tm, D_CONV), lambda i: (i, 0)),
            pl.BlockSpec((tm, D_REC), lambda i: (i, 0)),
            _layer_spec(layer, (D_MODEL, 2 * D_MODEL), col_block=2),
            _layer_spec(layer, (D_CONV, D_MODEL)),
            _layer_spec(layer, (D_REC, D_MODEL)),
        ],
        out_specs=pl.BlockSpec((tm, D_MODEL), lambda i: (i, 0)),
        out_shape=jax.ShapeDtypeStruct((m, D_MODEL), BF16),
        compiler_params=_params(1),
        name="merge",
    )(x2, g, ya, yb, w_in_bf, w_a_out, w_b_out)


def _out_ple_kernel(x_ref, m_ref, wo_ref, pg_ref, wpg_ref, p_ref, wpp_ref, fg_ref, o_ref,
                    *, final):
    o_ref[...] = x_ref[...] + _dot(m_ref[...], wo_ref[...])
    xn = _rmsnorm(o_ref[...], pg_ref[...]).astype(BF16)
    p = p_ref[...].astype(BF16)
    nb = MERGE_BLOCK
    for n in range(D_MODEL // nb):
        sl = slice(n * nb, (n + 1) * nb)
        gate = _sigmoid(_dot(xn, wpg_ref[:, sl]))
        o_ref[:, sl] = o_ref[:, sl] + gate * _dot(p, wpp_ref[:, sl])
    if final:
        o_ref[...] = _rmsnorm(o_ref[...], fg_ref[...])


def _out_ple(x2, m_act, w_o, ple_g, w_pg, p_l, w_pp, final_g, layer, final):
    m = x2.shape[0]
    tm = ROW_TILE
    kern = functools.partial(_out_ple_kernel, final=final)
    return pl.pallas_call(
        kern,
        grid=(m // tm,),
        in_specs=[
            pl.BlockSpec((tm, D_MODEL), lambda i: (i, 0)),
            pl.BlockSpec((tm, D_MODEL), lambda i: (i, 0)),
            _layer_spec(layer, (D_MODEL, D_MODEL)),
            pl.BlockSpec((None, 1, D_MODEL), lambda i: (layer, 0, 0)),
            _layer_spec(layer, (D_MODEL, D_MODEL)),
            pl.BlockSpec((None, tm, PLE_DIM), lambda i: (layer, i, 0)),
            _layer_spec(layer, (PLE_DIM, D_MODEL)),
            pl.BlockSpec((1, D_MODEL), lambda i: (0, 0)),
        ],
        out_specs=pl.BlockSpec((tm, D_MODEL), lambda i: (i, 0)),
        out_shape=jax.ShapeDtypeStruct((m, D_MODEL), F32),
        compiler_params=_params(1),
        name="out_ple",
    )(x2, m_act, w_o, ple_g, w_pg, p_l, w_pp, final_g)


def kernel(x, p, norm_mix_g, w_in, conv_w, lb_param, hg_norm_g, w_a_out, w_b_out, w_o,
           ple_norm_g, w_ple_gate, w_ple_proj, final_norm_g):
    batch, seq_len, d = x.shape
    depth = w_in.shape[0]
    m = batch * seq_len
    assert d == D_MODEL and seq_len % ROW_TILE == 0
    assert seq_len % MIX_TILE == 0 and MIX_TILE % CHUNK == 0
    assert w_in.shape[2] == 4 * D_CONV + 4 * D_REC + 2 * D_MODEL and 4 * D_CONV == 4 * D_REC

    w_in_bf = w_in.astype(BF16)
    w_a_out_bf = w_a_out.astype(BF16)
    w_b_out_bf = w_b_out.astype(BF16)
    w_o_bf = w_o.astype(BF16)
    w_pg_bf = w_ple_gate.astype(BF16)
    w_pp_bf = w_ple_proj.astype(BF16)
    mix_g = norm_mix_g.reshape(depth, 1, D_MODEL)
    ple_g = ple_norm_g.reshape(depth, 1, D_MODEL)
    hg_g = hg_norm_g.reshape(depth, 1, D_REC)
    final_g = final_norm_g.reshape(1, D_MODEL)
    p2 = p.reshape(depth, m, PLE_DIM)

    x2 = x.reshape(m, D_MODEL)
    for layer in range(depth):
        ya, yb = _mixers(x2, mix_g, w_in_bf, conv_w, lb_param, hg_g, layer, seq_len)
        m_act = _merge(x2, mix_g, ya, yb, w_in_bf, w_a_out_bf, w_b_out_bf, layer)
        x2 = _out_ple(x2, m_act, w_o_bf, ple_g, w_pg_bf, p2, w_pp_bf, final_g, layer,
                      final=(layer == depth - 1))
    return x2.reshape(batch, seq_len, D_MODEL)
```

```python
import functools
import math

import numpy as np
import jax
import jax.numpy as jnp
from jax import lax
from jax.experimental import pallas as pl
from jax.experimental.pallas import tpu as pltpu

D_MODEL = 2048
D_CONV = 1024
D_REC = 1024
HG_HEADS = 8
HG_DIM = 128
PLE_DIM = 256
CONV_K = 3
EPS = 1e-6
LB_FLOOR = 1e-30

ROW_TILE = 512
MIX_TILE = 256
COL_BLOCK = 256
MERGE_BLOCK = 512
CHUNK = 128
N_LEVELS = int(math.log2(CHUNK))
SUBLANES = 8
HALO_ROWS = 16
VMEM_LIMIT_BYTES = 56 * 1024 * 1024

F32 = jnp.float32
BF16 = jnp.bfloat16


def _dot(a, b):
    return jnp.dot(a, b, preferred_element_type=F32)


def _dot_tn(a, b):
    return lax.dot_general(a, b, (((0,), (0,)), ((), ())), preferred_element_type=F32)


def _rmsnorm(x, g):
    ms = jnp.mean(x * x, axis=-1, keepdims=True)
    return x * lax.rsqrt(ms + EPS) * g


def _sigmoid(x):
    return jax.nn.sigmoid(x)


def _silu(x):
    return x * jax.nn.sigmoid(x)


def _layer_spec(layer, shape, col_block=0):
    nd = len(shape)
    idx = (layer,) + (0,) * (nd - 1) + (col_block,)
    return pl.BlockSpec((None,) + tuple(shape), lambda *_: idx, pipeline_mode=pl.Buffered(1))


def _params(n_axes):
    return pltpu.CompilerParams(dimension_semantics=("arbitrary",) * n_axes,
                                vmem_limit_bytes=VMEM_LIMIT_BYTES)


def _level_table():
    c = CHUNK
    t = np.arange(c)[:, None]
    j = np.arange(c)[None, :]
    xor = np.maximum(t ^ j, 1)
    return np.where(j < t, np.floor(np.log2(xor)).astype(np.int32),
                    np.where(j == t, N_LEVELS, -1)).astype(np.int32)


def _bcast_row(ref, r, n):
    return jnp.broadcast_to(ref[r:r + 1, :], (n, HG_DIM))


def _level_operand(l, q, k, b, lf, b_ref, row):
    c = CHUNK
    h = 1 << l
    blk = 2 * h
    if l == 0:
        lower = (row & 1) != 0
        x = jnp.where(lower, lf, 0.0)
        m = jnp.where(lower, q, k)
    elif blk <= SUBLANES:
        lower = (row & h) != 0
        ref = None
        for sb in range(SUBLANES // blk):
            part = jnp.concatenate(
                [_bcast_row(b_ref, v0 + sb * blk + h - 1, SUBLANES)
                 for v0 in range(0, c, SUBLANES)], axis=0)
            ref = part if ref is None else jnp.where((row & (SUBLANES - 1)) < sb * blk, ref, part)
        x = jnp.where(lower, b - ref, ref - b)
        m = jnp.where(lower, q, k)
    else:
        xs, ms = [], []
        for b0 in range(0, c, blk):
            mid = _bcast_row(b_ref, b0 + h - 1, h)
            xs += [mid - b[b0:b0 + h], b[b0 + h:b0 + blk] - mid]
            ms += [k[b0:b0 + h], q[b0 + h:b0 + blk]]
        x = jnp.concatenate(xs, axis=0)
        m = jnp.concatenate(ms, axis=0)
    return (m * jnp.exp2(x)).astype(BF16)


OP_INTER = N_LEVELS
OP_STATE = N_LEVELS + 1
OP_DIAG = N_LEVELS + 2
N_OPERANDS = N_LEVELS + 3


def _mixers_kernel(x_ref, xh_ref, g_ref, wa_ref, wb_ref, cw_ref, lbp_ref, hg_ref, tril_ref,
                   lvl_ref, ya_ref, yb_ref,
                   q_s, k_s, lf_s, b_s, v_s, og_s, bc_s, op_s, opt_s, stt_s, dec_s, state_s,
                   *, layer, tiles_per_seq):
    i = pl.program_id(0)
    seq_start = (i % tiles_per_seq) == 0

    @pl.when(seq_start)
    def _():
        state_s[...] = jnp.zeros_like(state_s)

    lbp = lbp_ref[...]
    rows = [lbp[r:r + 1, :] for r in range(lbp.shape[0])]
    mx = functools.reduce(jnp.maximum, rows)
    ex = [jnp.exp(r - mx) for r in rows]
    den = functools.reduce(lambda a, b: a + b, ex)
    lb = jnp.zeros_like(mx)
    for r in range(1, layer + 1):
        lb = lb + ex[r] / den
    lb = jnp.clip(lb, 0.0, 1.0 - 1e-6)
    lb_floor = jnp.maximum(lb, LB_FLOOR)
    one_m_lb = 1.0 - lb

    c = CHUNK
    tm = x_ref.shape[0]
    n_chunks = tm // c
    g = g_ref[...]
    tril = tril_ref[...]
    h_ext = _rmsnorm(jnp.concatenate([xh_ref[...], x_ref[...]], axis=0), g).astype(BF16)
    h = h_ext[HALO_ROWS:]
    cb = COL_BLOCK
    heads_per_block = cb // HG_DIM

    for j in range(D_REC // cb):
        sl = slice(j * cb, (j + 1) * cb)
        u_q, u_f, u_i, u_g = [_dot(h, wb_ref[:, grp * D_REC + j * cb:grp * D_REC + (j + 1) * cb])
                              for grp in range(4)]
        qv = _silu(u_q)
        lfv = jnp.log2(lb_floor[:, sl] + one_m_lb[:, sl] * _sigmoid(u_f))
        kv = one_m_lb[:, sl] * _sigmoid(-u_f)
        ogv = _silu(u_g)
        for e in range(heads_per_block):
            head = j * heads_per_block + e
            es = slice(e * HG_DIM, (e + 1) * HG_DIM)
            q_s[head] = qv[:, es]
            k_s[head] = kv[:, es]
            lf_s[head] = lfv[:, es]
            v_s[head] = u_i[:, es].astype(BF16)
            og_s[head] = ogv[:, es]
        for ci in range(n_chunks):
            lfc = lfv[ci * c:(ci + 1) * c]
            hi = lfc.astype(BF16)
            rem = lfc - hi.astype(F32)
            mid = rem.astype(BF16)
            lo = (rem - mid.astype(F32)).astype(BF16)
            pre = _dot(tril, hi) + _dot(tril, mid) + _dot(tril, lo)
            for e in range(heads_per_block):
                b_s[j * heads_per_block + e, ci * c:(ci + 1) * c, :] = (
                    pre[:, e * HG_DIM:(e + 1) * HG_DIM])

    lvl = lvl_ref[...]
    row = lax.broadcasted_iota(jnp.int32, (c, HG_DIM), 0)
    row_a = lax.broadcasted_iota(jnp.int32, (tm, cb), 0)

    def prepare(ci, head):
        rs = slice(ci * c, (ci + 1) * c)
        q = q_s[head, rs, :]
        k = k_s[head, rs, :]
        lf = lf_s[head, rs, :]
        b = b_s[head, rs, :]
        bc = bc_s.at[head]
        bc[...] = b
        for l in range(N_LEVELS):
            op = _level_operand(l, q, k, b, lf, bc, row)
            op_s[head, l] = op
            opt_s[head, l] = op.T
        b_last = _bcast_row(bc, c - 1, c)
        op_s[head, OP_INTER] = (q * jnp.exp2(b)).astype(BF16)
        op_s[head, OP_STATE] = (k * jnp.exp2(b_last - b)).astype(BF16)
        diag = jnp.sum(q * k, axis=-1, keepdims=True)
        op_s[head, OP_DIAG] = jnp.broadcast_to(diag, (c, HG_DIM)).astype(BF16)
        dec_s[head] = jnp.exp2(b_last[0:SUBLANES, :])

    def contract_chunk(ci, filler_a, filler_b):
        rs = slice(ci * c, (ci + 1) * c)
        heads = range(HG_HEADS)
        states = [state_s[hd] for hd in heads]
        att = [jnp.where(lvl == N_LEVELS, op_s[hd, OP_DIAG], jnp.zeros((c, c), BF16))
               for hd in heads]
        for l in range(N_LEVELS):
            for hd in heads:
                p_l = _dot(op_s[hd, l], opt_s[hd, l])
                att[hd] = jnp.where(lvl == l, p_l.astype(BF16), att[hd])
        filler_a()
        for hd in heads:
            stt_s[hd] = states[hd].astype(BF16).T
        inter = [_dot(op_s[hd, OP_INTER], stt_s[hd]) for hd in heads]
        vals = [v_s[hd, rs, :] for hd in heads]
        for hd in heads:
            state_s[hd] = states[hd] * dec_s[hd][0:1, :] + _dot_tn(vals[hd], op_s[hd, OP_STATE])
        outs = [inter[hd] + _dot(att[hd], vals[hd]) for hd in heads]
        filler_b()
        for hd in heads:
            o = outs[hd]
            o = o * lax.rsqrt(jnp.mean(o * o, axis=-1, keepdims=True) + EPS) * hg_ref[hd]
            yb_ref[rs, hd * HG_DIM:(hd + 1) * HG_DIM] = (o * og_s[hd, rs, :]).astype(yb_ref.dtype)

    def conv_block(j):
        w_c, w_b, w_x, w_z = [wa_ref[:, grp * D_CONV + j * cb:grp * D_CONV + (j + 1) * cb]
                              for grp in range(4)]
        v_ext = _dot(h_ext, w_c) * _dot(h_ext, w_x)
        v = v_ext[HALO_ROWS:]
        vh = jnp.where(seq_start, 0.0, v_ext[:HALO_ROWS])
        p1 = vh[HALO_ROWS - 1:HALO_ROWS, :]
        p2 = vh[HALO_ROWS - 2:HALO_ROWS - 1, :]
        v1 = jnp.where(row_a == 0, p1, pltpu.roll(v, 1, 0))
        v2 = jnp.where(row_a == 0, p2, jnp.where(row_a == 1, p1, pltpu.roll(v, 2, 0)))
        cw = cw_ref[:, j * cb:(j + 1) * cb]
        conv = v2 * cw[0:1, :] + v1 * cw[1:2, :] + v * cw[2:3, :]
        ya = _dot(h, w_b) * conv * _silu(_dot(h, w_z))
        ya_ref[:, j * cb:(j + 1) * cb] = ya.astype(ya_ref.dtype)

    n_stages = 2 * n_chunks

    def conv_blocks(stage):
        def run():
            for j in range(stage, D_CONV // cb, n_stages):
                conv_block(j)
        return run

    for hd in range(HG_HEADS):
        prepare(0, hd)
    for ci in range(n_chunks):
        contract_chunk(ci, conv_blocks(2 * ci), conv_blocks(2 * ci + 1))
        if ci + 1 < n_chunks:
            for hd in range(HG_HEADS):
                prepare(ci + 1, hd)


def _mixers(x2, g, w_in_bf, conv_w, lb_param, hg_g, layer, seq_len):
    m = x2.shape[0]
    tm = MIX_TILE
    c = CHUNK
    tril = jnp.asarray(np.tril(np.ones((c, c), np.float32)), BF16)
    level = jnp.asarray(_level_table(), BF16)
    depth = lb_param.shape[0]
    hg_heads = hg_g.reshape(depth, HG_HEADS, 1, HG_DIM)
    halo_blocks_per_tile = tm // HALO_ROWS
    kern = functools.partial(_mixers_kernel, layer=layer, tiles_per_seq=seq_len // tm)
    const2 = lambda i: (0, 0)
    head_f32 = pltpu.VMEM((HG_HEADS, tm, HG_DIM), F32)
    head_bf16 = pltpu.VMEM((HG_HEADS, tm, HG_DIM), BF16)
    return pl.pallas_call(
        kern,
        grid=(m // tm,),
        in_specs=[
            pl.BlockSpec((tm, D_MODEL), lambda i: (i, 0)),
            pl.BlockSpec((HALO_ROWS, D_MODEL),
                         lambda i: (jnp.maximum(i * halo_blocks_per_tile - 1, 0), 0)),
            pl.BlockSpec((None, 1, D_MODEL), lambda i: (layer, 0, 0)),
            _layer_spec(layer, (D_MODEL, 4 * D_CONV), col_block=0),
            _layer_spec(layer, (D_MODEL, 4 * D_REC), col_block=1),
            pl.BlockSpec((None, CONV_K, D_CONV), lambda i: (layer, 0, 0)),
            pl.BlockSpec((depth, D_REC), const2),
            pl.BlockSpec((None, HG_HEADS, 1, HG_DIM), lambda i: (layer, 0, 0, 0)),
            pl.BlockSpec(tril.shape, const2),
            pl.BlockSpec(level.shape, const2),
        ],
        out_specs=[pl.BlockSpec((tm, D_CONV), lambda i: (i, 0)),
                   pl.BlockSpec((tm, D_REC), lambda i: (i, 0))],
        out_shape=[jax.ShapeDtypeStruct((m, D_CONV), BF16),
                   jax.ShapeDtypeStruct((m, D_REC), BF16)],
        scratch_shapes=[head_f32, head_f32, head_f32, head_f32, head_bf16, head_f32,
                        pltpu.VMEM((HG_HEADS, c, HG_DIM), F32),
                        pltpu.VMEM((HG_HEADS, N_OPERANDS, c, HG_DIM), BF16),
                        pltpu.VMEM((HG_HEADS, N_LEVELS, HG_DIM, c), BF16),
                        pltpu.VMEM((HG_HEADS, HG_DIM, HG_DIM), BF16),
                        pltpu.VMEM((HG_HEADS, SUBLANES, HG_DIM), F32),
                        pltpu.VMEM((HG_HEADS, HG_DIM, HG_DIM), F32)],
        compiler_params=_params(1),
        name="mixers",
    )(x2, x2, g, w_in_bf, w_in_bf, conv_w, lb_param, hg_heads, tril, level)


def _merge_kernel(x_ref, g_ref, ya_ref, yb_ref, wg_ref, wa_ref, wb_ref, m_ref):
    h = _rmsnorm(x_ref[...], g_ref[...]).astype(BF16)
    ya = ya_ref[...]
    yb = yb_ref[...]
    nb = MERGE_BLOCK
    for n in range(D_MODEL // nb):
        sl = slice(n * nb, (n + 1) * nb)
        gate_a = _dot(h, wg_ref[:, n * nb:(n + 1) * nb])
        gate_b = _dot(h, wg_ref[:, D_MODEL + n * nb:D_MODEL + (n + 1) * nb])
        m = (_sigmoid(gate_a) * _dot(ya, wa_ref[:, sl])
             + _sigmoid(gate_b) * _dot(yb, wb_ref[:, sl]))
        m_ref[:, sl] = m.astype(m_ref.dtype)


def _merge(x2, g, ya, yb, w_in_bf, w_a_out, w_b_out, layer):
    m = x2.shape[0]
    tm = ROW_TILE
    return pl.pallas_call(
        _merge_kernel,
        grid=(m // tm,),
        in_specs=[
            pl.BlockSpec((tm, D_MODEL), lambda i: (i, 0)),
            pl.BlockSpec((None, 1, D_MODEL), lambda i: (layer, 0, 0)),
            pl.BlockSpec((tm, D_CONV), lambda i: (i, 0)),
            pl.BlockSpec((tm, D_REC), lambda i: (i, 0)),
            _layer_spec(layer, (D_MODEL, 2 * D_MODEL), col_block=2),
            _layer_spec(layer, (D_CONV, D_MODEL)),
            _layer_spec(layer, (D_REC, D_MODEL)),
        ],
        out_specs=pl.BlockSpec((tm, D_MODEL), lambda i: (i, 0)),
        out_shape=jax.ShapeDtypeStruct((m, D_MODEL), BF16),
        compiler_params=_params(1),
        name="merge",
    )(x2, g, ya, yb, w_in_bf, w_a_out, w_b_out)


def _out_ple_kernel(x_ref, m_ref, wo_ref, pg_ref, wpg_ref, p_ref, wpp_ref, fg_ref, o_ref,
                    *, final):
    o_ref[...] = x_ref[...] + _dot(m_ref[...], wo_ref[...])
    xn = _rmsnorm(o_ref[...], pg_ref[...]).astype(BF16)
    p = p_ref[...].astype(BF16)
    nb = MERGE_BLOCK
    for n in range(D_MODEL // nb):
        sl = slice(n * nb, (n + 1) * nb)
        gate = _sigmoid(_dot(xn, wpg_ref[:, sl]))
        o_ref[:, sl] = o_ref[:, sl] + gate * _dot(p, wpp_ref[:, sl])
    if final:
        o_ref[...] = _rmsnorm(o_ref[...], fg_ref[...])


def _out_ple(x2, m_act, w_o, ple_g, w_pg, p_l, w_pp, final_g, layer, final):
    m = x2.shape[0]
    tm = ROW_TILE
    kern = functools.partial(_out_ple_kernel, final=final)
    return pl.pallas_call(
        kern,
        grid=(m // tm,),
        in_specs=[
            pl.BlockSpec((tm, D_MODEL), lambda i: (i, 0)),
            pl.BlockSpec((tm, D_MODEL), lambda i: (i, 0)),
            _layer_spec(layer, (D_MODEL, D_MODEL)),
            pl.BlockSpec((None, 1, D_MODEL), lambda i: (layer, 0, 0)),
            _layer_spec(layer, (D_MODEL, D_MODEL)),
            pl.BlockSpec((None, tm, PLE_DIM), lambda i: (layer, i, 0)),
            _layer_spec(layer, (PLE_DIM, D_MODEL)),
            pl.BlockSpec((1, D_MODEL), lambda i: (0, 0)),
        ],
        out_specs=pl.BlockSpec((tm, D_MODEL), lambda i: (i, 0)),
        out_shape=jax.ShapeDtypeStruct((m, D_MODEL), F32),
        compiler_params=_params(1),
        name="out_ple",
    )(x2, m_act, w_o, ple_g, w_pg, p_l, w_pp, final_g)


def kernel(x, p, norm_mix_g, w_in, conv_w, lb_param, hg_norm_g, w_a_out, w_b_out, w_o,
           ple_norm_g, w_ple_gate, w_ple_proj, final_norm_g):
    batch, seq_len, d = x.shape
    depth = w_in.shape[0]
    m = batch * seq_len
    assert d == D_MODEL and seq_len % ROW_TILE == 0
    assert seq_len % MIX_TILE == 0 and MIX_TILE % CHUNK == 0
    assert w_in.shape[2] == 4 * D_CONV + 4 * D_REC + 2 * D_MODEL and 4 * D_CONV == 4 * D_REC

    w_in_bf = w_in.astype(BF16)
    w_a_out_bf = w_a_out.astype(BF16)
    w_b_out_bf = w_b_out.astype(BF16)
    w_o_bf = w_o.astype(BF16)
    w_pg_bf = w_ple_gate.astype(BF16)
    w_pp_bf = w_ple_proj.astype(BF16)
    mix_g = norm_mix_g.reshape(depth, 1, D_MODEL)
    ple_g = ple_norm_g.reshape(depth, 1, D_MODEL)
    hg_g = hg_norm_g.reshape(depth, 1, D_REC)
    final_g = final_norm_g.reshape(1, D_MODEL)
    p2 = p.reshape(depth, m, PLE_DIM)

    x2 = x.reshape(m, D_MODEL)
    for layer in range(depth):
        ya, yb = _mixers(x2, mix_g, w_in_bf, conv_w, lb_param, hg_g, layer, seq_len)
        m_act = _merge(x2, mix_g, ya, yb, w_in_bf, w_a_out_bf, w_b_out_bf, layer)
        x2 = _out_ple(x2, m_act, w_o_bf, ple_g, w_pg_bf, p2, w_pp_bf, final_g, layer,
                      final=(layer == depth - 1))
    return x2.reshape(batch, seq_len, D_MODEL)
```

```python
import functools
import math

import numpy as np
import jax
import jax.numpy as jnp
from jax import lax
from jax.experimental import pallas as pl
from jax.experimental.pallas import tpu as pltpu

D_MODEL = 2048
D_CONV = 1024
D_REC = 1024
HG_HEADS = 8
HG_DIM = 128
PLE_DIM = 256
CONV_K = 3
EPS = 1e-6
LB_FLOOR = 1e-30

ROW_TILE = 512
MIX_TILE = 256
COL_BLOCK = 256
MERGE_BLOCK = 512
CHUNK = 128
N_LEVELS = int(math.log2(CHUNK))
SUBLANES = 8
HALO_ROWS = 16
STAGE_ROWS = 64
VMEM_LIMIT_BYTES = 56 * 1024 * 1024

F32 = jnp.float32
BF16 = jnp.bfloat16


def _dot(a, b):
    return jnp.dot(a, b, preferred_element_type=F32)


def _dot_tn(a, b):
    return lax.dot_general(a, b, (((0,), (0,)), ((), ())), preferred_element_type=F32)


def _rmsnorm(x, g):
    ms = jnp.mean(x * x, axis=-1, keepdims=True)
    return x * lax.rsqrt(ms + EPS) * g


def _sigmoid(x):
    return jax.nn.sigmoid(x)


def _silu(x):
    return x * jax.nn.sigmoid(x)


def _params(n_axes):
    return pltpu.CompilerParams(dimension_semantics=("arbitrary",) * n_axes,
                                vmem_limit_bytes=VMEM_LIMIT_BYTES)


def _stream_weight_bf16(w_hbm, layer, col0, dst_ref, stage_ref, sem):
    k, n = dst_ref.shape
    rows = stage_ref.shape[1]
    n_chunks = k // rows

    def chunk_copy(ci):
        slot = ci % 2
        r0 = pl.multiple_of(ci * rows, rows)
        return pltpu.make_async_copy(w_hbm.at[layer, pl.ds(r0, rows), pl.ds(col0, n)],
                                     stage_ref.at[slot, :, pl.ds(0, n)], sem.at[slot])

    chunk_copy(0).start()

    def body(ci, carry):
        @pl.when(ci + 1 < n_chunks)
        def _():
            chunk_copy(ci + 1).start()

        chunk_copy(ci).wait()
        r0 = pl.multiple_of(ci * rows, rows)
        dst_ref[pl.ds(r0, rows), :] = stage_ref[ci % 2, :, 0:n].astype(BF16)
        return carry

    lax.fori_loop(0, n_chunks, body, 0)


def _stage_scratch(width):
    return [pltpu.VMEM((2, STAGE_ROWS, width), F32), pltpu.SemaphoreType.DMA((2,))]


def _level_table():
    c = CHUNK
    t = np.arange(c)[:, None]
    j = np.arange(c)[None, :]
    xor = np.maximum(t ^ j, 1)
    return np.where(j < t, np.floor(np.log2(xor)).astype(np.int32),
                    np.where(j == t, N_LEVELS, -1)).astype(np.int32)


def _bcast_row(ref, r, n):
    return jnp.broadcast_to(ref[r:r + 1, :], (n, HG_DIM))


def _level_operand(l, q, k, b, lf, b_ref, row):
    c = CHUNK
    h = 1 << l
    blk = 2 * h
    if l == 0:
        lower = (row & 1) != 0
        x = jnp.where(lower, lf, 0.0)
        m = jnp.where(lower, q, k)
    elif blk <= SUBLANES:
        lower = (row & h) != 0
        ref = None
        for sb in range(SUBLANES // blk):
            part = jnp.concatenate(
                [_bcast_row(b_ref, v0 + sb * blk + h - 1, SUBLANES)
                 for v0 in range(0, c, SUBLANES)], axis=0)
            ref = part if ref is None else jnp.where((row & (SUBLANES - 1)) < sb * blk, ref, part)
        x = jnp.where(lower, b - ref, ref - b)
        m = jnp.where(lower, q, k)
    else:
        xs, ms = [], []
        for b0 in range(0, c, blk):
            mid = _bcast_row(b_ref, b0 + h - 1, h)
            xs += [mid - b[b0:b0 + h], b[b0 + h:b0 + blk] - mid]
            ms += [k[b0:b0 + h], q[b0 + h:b0 + blk]]
        x = jnp.concatenate(xs, axis=0)
        m = jnp.concatenate(ms, axis=0)
    return (m * jnp.exp2(x)).astype(BF16)


OP_INTER = N_LEVELS
OP_STATE = N_LEVELS + 1
OP_DIAG = N_LEVELS + 2
N_OPERANDS = N_LEVELS + 3


def _mixers_kernel(x_ref, xh_ref, g_ref, w_in_hbm, cw_ref, lbp_ref, hg_ref, tril_ref, lvl_ref,
                   ya_ref, yb_ref,
                   wa_ref, wb_ref, stage_s, sem,
                   q_s, k_s, lf_s, b_s, v_s, og_s, bc_s, op_s, opt_s, stt_s, dec_s, state_s,
                   *, layer, tiles_per_seq):
    i = pl.program_id(0)
    seq_start = (i % tiles_per_seq) == 0

    @pl.when(i == 0)
    def _():
        _stream_weight_bf16(w_in_hbm, layer, 0, wa_ref, stage_s, sem)
        _stream_weight_bf16(w_in_hbm, layer, 4 * D_CONV, wb_ref, stage_s, sem)

    @pl.when(seq_start)
    def _():
        state_s[...] = jnp.zeros_like(state_s)

    lbp = lbp_ref[...]
    rows = [lbp[r:r + 1, :] for r in range(lbp.shape[0])]
    mx = functools.reduce(jnp.maximum, rows)
    ex = [jnp.exp(r - mx) for r in rows]
    den = functools.reduce(lambda a, b: a + b, ex)
    lb = jnp.zeros_like(mx)
    for r in range(1, layer + 1):
        lb = lb + ex[r] / den
    lb = jnp.clip(lb, 0.0, 1.0 - 1e-6)
    lb_floor = jnp.maximum(lb, LB_FLOOR)
    one_m_lb = 1.0 - lb

    c = CHUNK
    tm = x_ref.shape[0]
    n_chunks = tm // c
    g = g_ref[...]
    tril = tril_ref[...]
    h_ext = _rmsnorm(jnp.concatenate([xh_ref[...], x_ref[...]], axis=0), g).astype(BF16)
    h = h_ext[HALO_ROWS:]
    cb = COL_BLOCK
    heads_per_block = cb // HG_DIM

    for j in range(D_REC // cb):
        sl = slice(j * cb, (j + 1) * cb)
        u_q, u_f, u_i, u_g = [_dot(h, wb_ref[:, grp * D_REC + j * cb:grp * D_REC + (j + 1) * cb])
                              for grp in range(4)]
        qv = _silu(u_q)
        lfv = jnp.log2(lb_floor[:, sl] + one_m_lb[:, sl] * _sigmoid(u_f))
        kv = one_m_lb[:, sl] * _sigmoid(-u_f)
        ogv = _silu(u_g)
        for e in range(heads_per_block):
            head = j * heads_per_block + e
            es = slice(e * HG_DIM, (e + 1) * HG_DIM)
            q_s[head] = qv[:, es]
            k_s[head] = kv[:, es]
            lf_s[head] = lfv[:, es]
            v_s[head] = u_i[:, es].astype(BF16)
            og_s[head] = ogv[:, es]
        for ci in range(n_chunks):
            lfc = lfv[ci * c:(ci + 1) * c]
            hi = lfc.astype(BF16)
            rem = lfc - hi.astype(F32)
            mid = rem.astype(BF16)
            lo = (rem - mid.astype(F32)).astype(BF16)
            pre = _dot(tril, hi) + _dot(tril, mid) + _dot(tril, lo)
            for e in range(heads_per_block):
                b_s[j * heads_per_block + e, ci * c:(ci + 1) * c, :] = (
                    pre[:, e * HG_DIM:(e + 1) * HG_DIM])

    lvl = lvl_ref[...]
    row = lax.broadcasted_iota(jnp.int32, (c, HG_DIM), 0)
    row_a = lax.broadcasted_iota(jnp.int32, (tm, cb), 0)

    def prepare(ci, head):
        rs = slice(ci * c, (ci + 1) * c)
        q = q_s[head, rs, :]
        k = k_s[head, rs, :]
        lf = lf_s[head, rs, :]
        b = b_s[head, rs, :]
        bc = bc_s.at[head]
        bc[...] = b
        for l in range(N_LEVELS):
            op = _level_operand(l, q, k, b, lf, bc, row)
            op_s[ci, head, l] = op
            opt_s[ci, head, l] = op.T
        b_last = _bcast_row(bc, c - 1, c)
        op_s[ci, head, OP_INTER] = (q * jnp.exp2(b)).astype(BF16)
        op_s[ci, head, OP_STATE] = (k * jnp.exp2(b_last - b)).astype(BF16)
        diag = jnp.sum(q * k, axis=-1, keepdims=True)
        op_s[ci, head, OP_DIAG] = jnp.broadcast_to(diag, (c, HG_DIM)).astype(BF16)
        dec_s[ci, head] = jnp.exp2(b_last[0:SUBLANES, :])

    def attend_chunk(ci):
        att = [jnp.where(lvl == N_LEVELS, op_s[ci, hd, OP_DIAG], jnp.zeros((c, c), BF16))
               for hd in range(HG_HEADS)]
        for l in range(N_LEVELS):
            for hd in range(HG_HEADS):
                p_l = _dot(op_s[ci, hd, l], opt_s[ci, hd, l])
                att[hd] = jnp.where(lvl == l, p_l.astype(BF16), att[hd])
        return att

    def recur_chunk(ci, att):
        rs = slice(ci * c, (ci + 1) * c)
        heads = range(HG_HEADS)
        states = [state_s[hd] for hd in heads]
        for hd in heads:
            stt_s[hd] = states[hd].astype(BF16).T
        inter = [_dot(op_s[ci, hd, OP_INTER], stt_s[hd]) for hd in heads]
        vals = [v_s[hd, rs, :] for hd in heads]
        for hd in heads:
            state_s[hd] = (states[hd] * dec_s[ci, hd][0:1, :]
                           + _dot_tn(vals[hd], op_s[ci, hd, OP_STATE]))
        outs = [inter[hd] + _dot(att[hd], vals[hd]) for hd in heads]
        for hd in heads:
            o = outs[hd]
            o = o * lax.rsqrt(jnp.mean(o * o, axis=-1, keepdims=True) + EPS) * hg_ref[hd]
            yb_ref[rs, hd * HG_DIM:(hd + 1) * HG_DIM] = (o * og_s[hd, rs, :]).astype(yb_ref.dtype)

    def conv_block(j):
        w_c, w_b, w_x, w_z = [wa_ref[:, grp * D_CONV + j * cb:grp * D_CONV + (j + 1) * cb]
                              for grp in range(4)]
        v_ext = _dot(h_ext, w_c) * _dot(h_ext, w_x)
        v = v_ext[HALO_ROWS:]
        vh = jnp.where(seq_start, 0.0, v_ext[:HALO_ROWS])
        p1 = vh[HALO_ROWS - 1:HALO_ROWS, :]
        p2 = vh[HALO_ROWS - 2:HALO_ROWS - 1, :]
        v1 = jnp.where(row_a == 0, p1, pltpu.roll(v, 1, 0))
        v2 = jnp.where(row_a == 0, p2, jnp.where(row_a == 1, p1, pltpu.roll(v, 2, 0)))
        cw = cw_ref[:, j * cb:(j + 1) * cb]
        conv = v2 * cw[0:1, :] + v1 * cw[1:2, :] + v * cw[2:3, :]
        ya = _dot(h, w_b) * conv * _silu(_dot(h, w_z))
        ya_ref[:, j * cb:(j + 1) * cb] = ya.astype(ya_ref.dtype)

    for ci in range(n_chunks):
        for hd in range(HG_HEADS):
            prepare(ci, hd)
    for j in range(D_CONV // cb):
        conv_block(j)
    atts = [attend_chunk(ci) for ci in range(n_chunks)]
    for ci in range(n_chunks):
        recur_chunk(ci, atts[ci])


def _mixers(x2, g, w_in, conv_w, lb_param, hg_g, layer, seq_len):
    m = x2.shape[0]
    tm = MIX_TILE
    c = CHUNK
    n_chunks = tm // c
    tril = jnp.asarray(np.tril(np.ones((c, c), np.float32)), BF16)
    level = jnp.asarray(_level_table(), BF16)
    depth = lb_param.shape[0]
    hg_heads = hg_g.reshape(depth, HG_HEADS, 1, HG_DIM)
    halo_blocks_per_tile = tm // HALO_ROWS
    kern = functools.partial(_mixers_kernel, layer=layer, tiles_per_seq=seq_len // tm)
    const2 = lambda i: (0, 0)
    head_f32 = pltpu.VMEM((HG_HEADS, tm, HG_DIM), F32)
    head_bf16 = pltpu.VMEM((HG_HEADS, tm, HG_DIM), BF16)
    return pl.pallas_call(
        kern,
        grid=(m // tm,),
        in_specs=[
            pl.BlockSpec((tm, D_MODEL), lambda i: (i, 0)),
            pl.BlockSpec((HALO_ROWS, D_MODEL),
                         lambda i: (jnp.maximum(i * halo_blocks_per_tile - 1, 0), 0)),
            pl.BlockSpec((None, 1, D_MODEL), lambda i: (layer, 0, 0)),
            pl.BlockSpec(memory_space=pl.ANY),
            pl.BlockSpec((None, CONV_K, D_CONV), lambda i: (layer, 0, 0)),
            pl.BlockSpec((depth, D_REC), const2),
            pl.BlockSpec((None, HG_HEADS, 1, HG_DIM), lambda i: (layer, 0, 0, 0)),
            pl.BlockSpec(tril.shape, const2),
            pl.BlockSpec(level.shape, const2),
        ],
        out_specs=[pl.BlockSpec((tm, D_CONV), lambda i: (i, 0)),
                   pl.BlockSpec((tm, D_REC), lambda i: (i, 0))],
        out_shape=[jax.ShapeDtypeStruct((m, D_CONV), BF16),
                   jax.ShapeDtypeStruct((m, D_REC), BF16)],
        scratch_shapes=[pltpu.VMEM((D_MODEL, 4 * D_CONV), BF16),
                        pltpu.VMEM((D_MODEL, 4 * D_REC), BF16)]
        + _stage_scratch(4 * D_CONV)
        + [head_f32, head_f32, head_f32, head_f32, head_bf16, head_f32,
           pltpu.VMEM((HG_HEADS, c, HG_DIM), F32),
           pltpu.VMEM((n_chunks, HG_HEADS, N_OPERANDS, c, HG_DIM), BF16),
           pltpu.VMEM((n_chunks, HG_HEADS, N_LEVELS, HG_DIM, c), BF16),
           pltpu.VMEM((HG_HEADS, HG_DIM, HG_DIM), BF16),
           pltpu.VMEM((n_chunks, HG_HEADS, SUBLANES, HG_DIM), F32),
           pltpu.VMEM((HG_HEADS, HG_DIM, HG_DIM), F32)],
        compiler_params=_params(1),
        name="mixers",
    )(x2, x2, g, w_in, conv_w, lb_param, hg_heads, tril, level)


def _merge_kernel(x_ref, g_ref, ya_ref, yb_ref, w_in_hbm, wa_hbm, wb_hbm, m_ref,
                  wg_ref, wa_ref, wb_ref, stage_s, sem, *, layer):
    @pl.when(pl.program_id(0) == 0)
    def _():
        _stream_weight_bf16(w_in_hbm, layer, 4 * D_CONV + 4 * D_REC, wg_ref, stage_s, sem)
        _stream_weight_bf16(wa_hbm, layer, 0, wa_ref, stage_s, sem)
        _stream_weight_bf16(wb_hbm, layer, 0, wb_ref, stage_s, sem)

    h = _rmsnorm(x_ref[...], g_ref[...]).astype(BF16)
    ya = ya_ref[...]
    yb = yb_ref[...]
    nb = MERGE_BLOCK
    for n in range(D_MODEL // nb):
        sl = slice(n * nb, (n + 1) * nb)
        gate_a = _dot(h, wg_ref[:, n * nb:(n + 1) * nb])
        gate_b = _dot(h, wg_ref[:, D_MODEL + n * nb:D_MODEL + (n + 1) * nb])
        m = (_sigmoid(gate_a) * _dot(ya, wa_ref[:, sl])
             + _sigmoid(gate_b) * _dot(yb, wb_ref[:, sl]))
        m_ref[:, sl] = m.astype(m_ref.dtype)


def _merge(x2, g, ya, yb, w_in, w_a_out, w_b_out, layer):
    m = x2.shape[0]
    tm = ROW_TILE
    return pl.pallas_call(
        functools.partial(_merge_kernel, layer=layer),
        grid=(m // tm,),
        in_specs=[
            pl.BlockSpec((tm, D_MODEL), lambda i: (i, 0)),
            pl.BlockSpec((None, 1, D_MODEL), lambda i: (layer, 0, 0)),
            pl.BlockSpec((tm, D_CONV), lambda i: (i, 0)),
            pl.BlockSpec((tm, D_REC), lambda i: (i, 0)),
            pl.BlockSpec(memory_space=pl.ANY),
            pl.BlockSpec(memory_space=pl.ANY),
            pl.BlockSpec(memory_space=pl.ANY),
        ],
        out_specs=pl.BlockSpec((tm, D_MODEL), lambda i: (i, 0)),
        out_shape=jax.ShapeDtypeStruct((m, D_MODEL), BF16),
        scratch_shapes=[pltpu.VMEM((D_MODEL, 2 * D_MODEL), BF16),
                        pltpu.VMEM((D_CONV, D_MODEL), BF16),
                        pltpu.VMEM((D_REC, D_MODEL), BF16)]
        + _stage_scratch(2 * D_MODEL),
        compiler_params=_params(1),
        name="merge",
    )(x2, g, ya, yb, w_in, w_a_out, w_b_out)


def _out_ple_kernel(x_ref, m_ref, wo_hbm, pg_ref, wpg_hbm, p_ref, wpp_hbm, fg_ref, o_ref,
                    wo_ref, wpg_ref, wpp_ref, stage_s, sem, *, layer, final):
    @pl.when(pl.program_id(0) == 0)
    def _():
        _stream_weight_bf16(wo_hbm, layer, 0, wo_ref, stage_s, sem)
        _stream_weight_bf16(wpg_hbm, layer, 0, wpg_ref, stage_s, sem)
        _stream_weight_bf16(wpp_hbm, layer, 0, wpp_ref, stage_s, sem)

    o_ref[...] = x_ref[...] + _dot(m_ref[...], wo_ref[...])
    xn = _rmsnorm(o_ref[...], pg_ref[...]).astype(BF16)
    p = p_ref[...].astype(BF16)
    nb = MERGE_BLOCK
    for n in range(D_MODEL // nb):
        sl = slice(n * nb, (n + 1) * nb)
        gate = _sigmoid(_dot(xn, wpg_ref[:, sl]))
        o_ref[:, sl] = o_ref[:, sl] + gate * _dot(p, wpp_ref[:, sl])
    if final:
        o_ref[...] = _rmsnorm(o_ref[...], fg_ref[...])


def _out_ple(x2, m_act, w_o, ple_g, w_pg, p_l, w_pp, final_g, layer, final):
    m = x2.shape[0]
    tm = ROW_TILE
    kern = functools.partial(_out_ple_kernel, layer=layer, final=final)
    return pl.pallas_call(
        kern,
        grid=(m // tm,),
        in_specs=[
            pl.BlockSpec((tm, D_MODEL), lambda i: (i, 0)),
            pl.BlockSpec((tm, D_MODEL), lambda i: (i, 0)),
            pl.BlockSpec(memory_space=pl.ANY),
            pl.BlockSpec((None, 1, D_MODEL), lambda i: (layer, 0, 0)),
            pl.BlockSpec(memory_space=pl.ANY),
            pl.BlockSpec((None, tm, PLE_DIM), lambda i: (layer, i, 0)),
            pl.BlockSpec(memory_space=pl.ANY),
            pl.BlockSpec((1, D_MODEL), lambda i: (0, 0)),
        ],
        out_specs=pl.BlockSpec((tm, D_MODEL), lambda i: (i, 0)),
        out_shape=jax.ShapeDtypeStruct((m, D_MODEL), F32),
        scratch_shapes=[pltpu.VMEM((D_MODEL, D_MODEL), BF16),
                        pltpu.VMEM((D_MODEL, D_MODEL), BF16),
                        pltpu.VMEM((PLE_DIM, D_MODEL), BF16)]
        + _stage_scratch(D_MODEL),
        compiler_params=_params(1),
        name="out_ple",
    )(x2, m_act, w_o, ple_g, w_pg, p_l, w_pp, final_g)


def kernel(x, p, norm_mix_g, w_in, conv_w, lb_param, hg_norm_g, w_a_out, w_b_out, w_o,
           ple_norm_g, w_ple_gate, w_ple_proj, final_norm_g):
    batch, seq_len, d = x.shape
    depth = w_in.shape[0]
    m = batch * seq_len
    assert d == D_MODEL and seq_len % ROW_TILE == 0
    assert seq_len % MIX_TILE == 0 and MIX_TILE % CHUNK == 0
    assert w_in.shape[2] == 4 * D_CONV + 4 * D_REC + 2 * D_MODEL and D_CONV == D_REC

    mix_g = norm_mix_g.reshape(depth, 1, D_MODEL)
    ple_g = ple_norm_g.reshape(depth, 1, D_MODEL)
    hg_g = hg_norm_g.reshape(depth, 1, D_REC)
    final_g = final_norm_g.reshape(1, D_MODEL)
    p2 = p.reshape(depth, m, PLE_DIM)

    x2 = x.reshape(m, D_MODEL)
    for layer in range(depth):
        ya, yb = _mixers(x2, mix_g, w_in, conv_w, lb_param, hg_g, layer, seq_len)
        m_act = _merge(x2, mix_g, ya, yb, w_in, w_a_out, w_b_out, layer)
        x2 = _out_ple(x2, m_act, w_o, ple_g, w_ple_gate, p2, w_ple_proj, final_g, layer,
                      final=(layer == depth - 1))
    return x2.reshape(batch, seq_len, D_MODEL)
```

```python
import functools
import math

import numpy as np
import jax
import jax.numpy as jnp
from jax import lax
from jax.experimental import pallas as pl
from jax.experimental.pallas import tpu as pltpu

D_MODEL = 2048
D_CONV = 1024
D_REC = 1024
HG_HEADS = 8
HG_DIM = 128
PLE_DIM = 256
CONV_K = 3
EPS = 1e-6
LB_FLOOR = 1e-30

ROW_TILE = 512
MIX_TILE = 256
COL_BLOCK = 256
MERGE_BLOCK = 512
CHUNK = 128
N_LEVELS = int(math.log2(CHUNK))
SUBLANES = 8
HALO_ROWS = 16
STAGE_CHUNK_BYTES = 1024 * 1024
MIX_STAGE_SLOTS = 4
STAGE_SLOTS = 6
VMEM_LIMIT_BYTES = 58 * 1024 * 1024

F32 = jnp.float32
BF16 = jnp.bfloat16


def _dot(a, b):
    return jnp.dot(a, b, preferred_element_type=F32)


def _dot_tn(a, b):
    return lax.dot_general(a, b, (((0,), (0,)), ((), ())), preferred_element_type=F32)


def _rmsnorm(x, g):
    ms = jnp.mean(x * x, axis=-1, keepdims=True)
    return x * lax.rsqrt(ms + EPS) * g


def _sigmoid(x):
    return jax.nn.sigmoid(x)


def _silu(x):
    return x * jax.nn.sigmoid(x)


def _params(n_axes):
    return pltpu.CompilerParams(dimension_semantics=("arbitrary",) * n_axes,
                                vmem_limit_bytes=VMEM_LIMIT_BYTES)


def _stream_weight_bf16(w_hbm, layer, col0, dst_ref, stage_ref, sem):
    k, n = dst_ref.shape
    slots, rows = stage_ref.shape[0], min(stage_ref.shape[1], k)
    assert k % rows == 0
    n_chunks = k // rows
    ahead = slots - 1

    def chunk_copy(ci):
        slot = ci % slots
        r0 = pl.multiple_of(ci * rows, rows)
        return pltpu.make_async_copy(w_hbm.at[layer, pl.ds(r0, rows), pl.ds(col0, n)],
                                     stage_ref.at[slot, pl.ds(0, rows), pl.ds(0, n)],
                                     sem.at[slot])

    for ci in range(min(ahead, n_chunks)):
        chunk_copy(ci).start()

    def body(ci, carry):
        @pl.when(ci + ahead < n_chunks)
        def _():
            chunk_copy(ci + ahead).start()

        chunk_copy(ci).wait()
        r0 = pl.multiple_of(ci * rows, rows)
        dst_ref[pl.ds(r0, rows), :] = stage_ref[ci % slots, 0:rows, 0:n].astype(BF16)
        return carry

    lax.fori_loop(0, n_chunks, body, 0)


def _stage_scratch(width, slots):
    rows = STAGE_CHUNK_BYTES // (4 * width)
    return [pltpu.VMEM((slots, rows, width), F32), pltpu.SemaphoreType.DMA((slots,))]


def _level_table():
    c = CHUNK
    t = np.arange(c)[:, None]
    j = np.arange(c)[None, :]
    xor = np.maximum(t ^ j, 1)
    return np.where(j < t, np.floor(np.log2(xor)).astype(np.int32),
                    np.where(j == t, N_LEVELS, -1)).astype(np.int32)


def _bcast_row(ref, r, n):
    return jnp.broadcast_to(ref[r:r + 1, :], (n, HG_DIM))


def _level_operand(l, q, k, b, lf, b_ref, row):
    c = CHUNK
    h = 1 << l
    blk = 2 * h
    if l == 0:
        lower = (row & 1) != 0
        x = jnp.where(lower, lf, 0.0)
        m = jnp.where(lower, q, k)
    elif blk <= SUBLANES:
        lower = (row & h) != 0
        ref = None
        for sb in range(SUBLANES // blk):
            part = jnp.concatenate(
                [_bcast_row(b_ref, v0 + sb * blk + h - 1, SUBLANES)
                 for v0 in range(0, c, SUBLANES)], axis=0)
            ref = part if ref is None else jnp.where((row & (SUBLANES - 1)) < sb * blk, ref, part)
        x = jnp.where(lower, b - ref, ref - b)
        m = jnp.where(lower, q, k)
    else:
        xs, ms = [], []
        for b0 in range(0, c, blk):
            mid = _bcast_row(b_ref, b0 + h - 1, h)
            xs += [mid - b[b0:b0 + h], b[b0 + h:b0 + blk] - mid]
            ms += [k[b0:b0 + h], q[b0 + h:b0 + blk]]
        x = jnp.concatenate(xs, axis=0)
        m = jnp.concatenate(ms, axis=0)
    return (m * jnp.exp2(x)).astype(BF16)


OP_INTER = N_LEVELS
OP_STATE = N_LEVELS + 1
OP_DIAG = N_LEVELS + 2
N_OPERANDS = N_LEVELS + 3


def _mixers_kernel(x_ref, xh_ref, g_ref, w_in_hbm, cw_ref, lbp_ref, hg_ref, tril_ref, lvl_ref,
                   ya_ref, yb_ref,
                   wa_ref, wb_ref, stage_s, sem,
                   q_s, k_s, lf_s, b_s, v_s, og_s, bc_s, op_s, opt_s, stt_s, dec_s, state_s,
                   *, layer, tiles_per_seq):
    i = pl.program_id(0)
    seq_start = (i % tiles_per_seq) == 0

    @pl.when(i == 0)
    def _():
        _stream_weight_bf16(w_in_hbm, layer, 0, wa_ref, stage_s, sem)
        _stream_weight_bf16(w_in_hbm, layer, 4 * D_CONV, wb_ref, stage_s, sem)

    @pl.when(seq_start)
    def _():
        state_s[...] = jnp.zeros_like(state_s)

    lbp = lbp_ref[...]
    rows = [lbp[r:r + 1, :] for r in range(lbp.shape[0])]
    mx = functools.reduce(jnp.maximum, rows)
    ex = [jnp.exp(r - mx) for r in rows]
    den = functools.reduce(lambda a, b: a + b, ex)
    lb = jnp.zeros_like(mx)
    for r in range(1, layer + 1):
        lb = lb + ex[r] / den
    lb = jnp.clip(lb, 0.0, 1.0 - 1e-6)
    lb_floor = jnp.maximum(lb, LB_FLOOR)
    one_m_lb = 1.0 - lb

    c = CHUNK
    tm = x_ref.shape[0]
    n_chunks = tm // c
    g = g_ref[...]
    tril = tril_ref[...]
    h_ext = _rmsnorm(jnp.concatenate([xh_ref[...], x_ref[...]], axis=0), g).astype(BF16)
    h = h_ext[HALO_ROWS:]
    cb = COL_BLOCK
    heads_per_block = cb // HG_DIM

    for j in range(D_REC // cb):
        sl = slice(j * cb, (j + 1) * cb)
        u_q, u_f, u_i, u_g = [_dot(h, wb_ref[:, grp * D_REC + j * cb:grp * D_REC + (j + 1) * cb])
                              for grp in range(4)]
        qv = _silu(u_q)
        lfv = jnp.log2(lb_floor[:, sl] + one_m_lb[:, sl] * _sigmoid(u_f))
        kv = one_m_lb[:, sl] * _sigmoid(-u_f)
        ogv = _silu(u_g)
        for e in range(heads_per_block):
            head = j * heads_per_block + e
            es = slice(e * HG_DIM, (e + 1) * HG_DIM)
            q_s[head] = qv[:, es]
            k_s[head] = kv[:, es]
            lf_s[head] = lfv[:, es]
            v_s[head] = u_i[:, es].astype(BF16)
            og_s[head] = ogv[:, es]
        for ci in range(n_chunks):
            lfc = lfv[ci * c:(ci + 1) * c]
            hi = lfc.astype(BF16)
            rem = lfc - hi.astype(F32)
            mid = rem.astype(BF16)
            lo = (rem - mid.astype(F32)).astype(BF16)
            pre = _dot(tril, hi) + _dot(tril, mid) + _dot(tril, lo)
            for e in range(heads_per_block):
                b_s[j * heads_per_block + e, ci * c:(ci + 1) * c, :] = (
                    pre[:, e * HG_DIM:(e + 1) * HG_DIM])

    lvl = lvl_ref[...]
    row = lax.broadcasted_iota(jnp.int32, (c, HG_DIM), 0)
    row_a = lax.broadcasted_iota(jnp.int32, (tm, cb), 0)

    def prepare(ci, head):
        rs = slice(ci * c, (ci + 1) * c)
        q = q_s[head, rs, :]
        k = k_s[head, rs, :]
        lf = lf_s[head, rs, :]
        b = b_s[head, rs, :]
        bc = bc_s.at[head]
        bc[...] = b
        for l in range(N_LEVELS):
            op = _level_operand(l, q, k, b, lf, bc, row)
            op_s[ci, head, l] = op
            opt_s[ci, head, l] = op.T
        b_last = _bcast_row(bc, c - 1, c)
        op_s[ci, head, OP_INTER] = (q * jnp.exp2(b)).astype(BF16)
        op_s[ci, head, OP_STATE] = (k * jnp.exp2(b_last - b)).astype(BF16)
        diag = jnp.sum(q * k, axis=-1, keepdims=True)
        op_s[ci, head, OP_DIAG] = jnp.broadcast_to(diag, (c, HG_DIM)).astype(BF16)
        dec_s[ci, head] = jnp.exp2(b_last[0:SUBLANES, :])

    def attend_chunk(ci):
        att = [jnp.where(lvl == N_LEVELS, op_s[ci, hd, OP_DIAG], jnp.zeros((c, c), BF16))
               for hd in range(HG_HEADS)]
        for l in range(N_LEVELS):
            for hd in range(HG_HEADS):
                p_l = _dot(op_s[ci, hd, l], opt_s[ci, hd, l])
                att[hd] = jnp.where(lvl == l, p_l.astype(BF16), att[hd])
        return att

    def recur_chunk(ci, att):
        rs = slice(ci * c, (ci + 1) * c)
        heads = range(HG_HEADS)
        states = [state_s[hd] for hd in heads]
        for hd in heads:
            stt_s[hd] = states[hd].astype(BF16).T
        inter = [_dot(op_s[ci, hd, OP_INTER], stt_s[hd]) for hd in heads]
        vals = [v_s[hd, rs, :] for hd in heads]
        for hd in heads:
            state_s[hd] = (states[hd] * dec_s[ci, hd][0:1, :]
                           + _dot_tn(vals[hd], op_s[ci, hd, OP_STATE]))
        outs = [inter[hd] + _dot(att[hd], vals[hd]) for hd in heads]
        for hd in heads:
            o = outs[hd]
            o = o * lax.rsqrt(jnp.mean(o * o, axis=-1, keepdims=True) + EPS) * hg_ref[hd]
            yb_ref[rs, hd * HG_DIM:(hd + 1) * HG_DIM] = (o * og_s[hd, rs, :]).astype(yb_ref.dtype)

    def conv_block(j):
        w_c, w_b, w_x, w_z = [wa_ref[:, grp * D_CONV + j * cb:grp * D_CONV + (j + 1) * cb]
                              for grp in range(4)]
        v_ext = _dot(h_ext, w_c) * _dot(h_ext, w_x)
        v = v_ext[HALO_ROWS:]
        vh = jnp.where(seq_start, 0.0, v_ext[:HALO_ROWS])
        p1 = vh[HALO_ROWS - 1:HALO_ROWS, :]
        p2 = vh[HALO_ROWS - 2:HALO_ROWS - 1, :]
        v1 = jnp.where(row_a == 0, p1, pltpu.roll(v, 1, 0))
        v2 = jnp.where(row_a == 0, p2, jnp.where(row_a == 1, p1, pltpu.roll(v, 2, 0)))
        cw = cw_ref[:, j * cb:(j + 1) * cb]
        conv = v2 * cw[0:1, :] + v1 * cw[1:2, :] + v * cw[2:3, :]
        ya = _dot(h, w_b) * conv * _silu(_dot(h, w_z))
        ya_ref[:, j * cb:(j + 1) * cb] = ya.astype(ya_ref.dtype)

    for ci in range(n_chunks):
        for hd in range(HG_HEADS):
            prepare(ci, hd)
    for j in range(D_CONV // cb):
        conv_block(j)
    atts = [attend_chunk(ci) for ci in range(n_chunks)]
    for ci in range(n_chunks):
        recur_chunk(ci, atts[ci])


def _mixers(x2, g, w_in, conv_w, lb_param, hg_g, layer, seq_len):
    m = x2.shape[0]
    tm = MIX_TILE
    c = CHUNK
    n_chunks = tm // c
    tril = jnp.asarray(np.tril(np.ones((c, c), np.float32)), BF16)
    level = jnp.asarray(_level_table(), BF16)
    depth = lb_param.shape[0]
    hg_heads = hg_g.reshape(depth, HG_HEADS, 1, HG_DIM)
    halo_blocks_per_tile = tm // HALO_ROWS
    kern = functools.partial(_mixers_kernel, layer=layer, tiles_per_seq=seq_len // tm)
    const2 = lambda i: (0, 0)
    head_f32 = pltpu.VMEM((HG_HEADS, tm, HG_DIM), F32)
    head_bf16 = pltpu.VMEM((HG_HEADS, tm, HG_DIM), BF16)
    return pl.pallas_call(
        kern,
        grid=(m // tm,),
        in_specs=[
            pl.BlockSpec((tm, D_MODEL), lambda i: (i, 0)),
            pl.BlockSpec((HALO_ROWS, D_MODEL),
                         lambda i: (jnp.maximum(i * halo_blocks_per_tile - 1, 0), 0)),
            pl.BlockSpec((None, 1, D_MODEL), lambda i: (layer, 0, 0)),
            pl.BlockSpec(memory_space=pl.ANY),
            pl.BlockSpec((None, CONV_K, D_CONV), lambda i: (layer, 0, 0)),
            pl.BlockSpec((depth, D_REC), const2),
            pl.BlockSpec((None, HG_HEADS, 1, HG_DIM), lambda i: (layer, 0, 0, 0)),
            pl.BlockSpec(tril.shape, const2),
            pl.BlockSpec(level.shape, const2),
        ],
        out_specs=[pl.BlockSpec((tm, D_CONV), lambda i: (i, 0)),
                   pl.BlockSpec((tm, D_REC), lambda i: (i, 0))],
        out_shape=[jax.ShapeDtypeStruct((m, D_CONV), BF16),
                   jax.ShapeDtypeStruct((m, D_REC), BF16)],
        scratch_shapes=[pltpu.VMEM((D_MODEL, 4 * D_CONV), BF16),
                        pltpu.VMEM((D_MODEL, 4 * D_REC), BF16)]
        + _stage_scratch(4 * D_CONV, MIX_STAGE_SLOTS)
        + [head_f32, head_f32, head_f32, head_f32, head_bf16, head_f32,
           pltpu.VMEM((HG_HEADS, c, HG_DIM), F32),
           pltpu.VMEM((n_chunks, HG_HEADS, N_OPERANDS, c, HG_DIM), BF16),
           pltpu.VMEM((n_chunks, HG_HEADS, N_LEVELS, HG_DIM, c), BF16),
           pltpu.VMEM((HG_HEADS, HG_DIM, HG_DIM), BF16),
           pltpu.VMEM((n_chunks, HG_HEADS, SUBLANES, HG_DIM), F32),
           pltpu.VMEM((HG_HEADS, HG_DIM, HG_DIM), F32)],
        compiler_params=_params(1),
        name="mixers",
    )(x2, x2, g, w_in, conv_w, lb_param, hg_heads, tril, level)


def _merge_kernel(x_ref, g_ref, ya_ref, yb_ref, w_in_hbm, wa_hbm, wb_hbm, m_ref,
                  wg_ref, wa_ref, wb_ref, stage_s, sem, *, layer):
    @pl.when(pl.program_id(0) == 0)
    def _():
        _stream_weight_bf16(w_in_hbm, layer, 4 * D_CONV + 4 * D_REC, wg_ref, stage_s, sem)
        _stream_weight_bf16(wa_hbm, layer, 0, wa_ref, stage_s, sem)
        _stream_weight_bf16(wb_hbm, layer, 0, wb_ref, stage_s, sem)

    h = _rmsnorm(x_ref[...], g_ref[...]).astype(BF16)
    ya = ya_ref[...]
    yb = yb_ref[...]
    nb = MERGE_BLOCK
    for n in range(D_MODEL // nb):
        sl = slice(n * nb, (n + 1) * nb)
        gate_a = _dot(h, wg_ref[:, n * nb:(n + 1) * nb])
        gate_b = _dot(h, wg_ref[:, D_MODEL + n * nb:D_MODEL + (n + 1) * nb])
        m = (_sigmoid(gate_a) * _dot(ya, wa_ref[:, sl])
             + _sigmoid(gate_b) * _dot(yb, wb_ref[:, sl]))
        m_ref[:, sl] = m.astype(m_ref.dtype)


def _merge(x2, g, ya, yb, w_in, w_a_out, w_b_out, layer):
    m = x2.shape[0]
    tm = ROW_TILE
    return pl.pallas_call(
        functools.partial(_merge_kernel, layer=layer),
        grid=(m // tm,),
        in_specs=[
            pl.BlockSpec((tm, D_MODEL), lambda i: (i, 0)),
            pl.BlockSpec((None, 1, D_MODEL), lambda i: (layer, 0, 0)),
            pl.BlockSpec((tm, D_CONV), lambda i: (i, 0)),
            pl.BlockSpec((tm, D_REC), lambda i: (i, 0)),
            pl.BlockSpec(memory_space=pl.ANY),
            pl.BlockSpec(memory_space=pl.ANY),
            pl.BlockSpec(memory_space=pl.ANY),
        ],
        out_specs=pl.BlockSpec((tm, D_MODEL), lambda i: (i, 0)),
        out_shape=jax.ShapeDtypeStruct((m, D_MODEL), BF16),
        scratch_shapes=[pltpu.VMEM((D_MODEL, 2 * D_MODEL), BF16),
                        pltpu.VMEM((D_CONV, D_MODEL), BF16),
                        pltpu.VMEM((D_REC, D_MODEL), BF16)]
        + _stage_scratch(2 * D_MODEL, STAGE_SLOTS),
        compiler_params=_params(1),
        name="merge",
    )(x2, g, ya, yb, w_in, w_a_out, w_b_out)


def _out_ple_kernel(x_ref, m_ref, wo_hbm, pg_ref, wpg_hbm, p_ref, wpp_hbm, fg_ref, o_ref,
                    wo_ref, wpg_ref, wpp_ref, stage_s, sem, *, layer, final):
    @pl.when(pl.program_id(0) == 0)
    def _():
        _stream_weight_bf16(wo_hbm, layer, 0, wo_ref, stage_s, sem)
        _stream_weight_bf16(wpg_hbm, layer, 0, wpg_ref, stage_s, sem)
        _stream_weight_bf16(wpp_hbm, layer, 0, wpp_ref, stage_s, sem)

    o_ref[...] = x_ref[...] + _dot(m_ref[...], wo_ref[...])
    xn = _rmsnorm(o_ref[...], pg_ref[...]).astype(BF16)
    p = p_ref[...].astype(BF16)
    nb = MERGE_BLOCK
    for n in range(D_MODEL // nb):
        sl = slice(n * nb, (n + 1) * nb)
        gate = _sigmoid(_dot(xn, wpg_ref[:, sl]))
        o_ref[:, sl] = o_ref[:, sl] + gate * _dot(p, wpp_ref[:, sl])
    if final:
        o_ref[...] = _rmsnorm(o_ref[...], fg_ref[...])


def _out_ple(x2, m_act, w_o, ple_g, w_pg, p_l, w_pp, final_g, layer, final):
    m = x2.shape[0]
    tm = ROW_TILE
    kern = functools.partial(_out_ple_kernel, layer=layer, final=final)
    return pl.pallas_call(
        kern,
        grid=(m // tm,),
        in_specs=[
            pl.BlockSpec((tm, D_MODEL), lambda i: (i, 0)),
            pl.BlockSpec((tm, D_MODEL), lambda i: (i, 0)),
            pl.BlockSpec(memory_space=pl.ANY),
            pl.BlockSpec((None, 1, D_MODEL), lambda i: (layer, 0, 0)),
            pl.BlockSpec(memory_space=pl.ANY),
            pl.BlockSpec((None, tm, PLE_DIM), lambda i: (layer, i, 0)),
            pl.BlockSpec(memory_space=pl.ANY),
            pl.BlockSpec((1, D_MODEL), lambda i: (0, 0)),
        ],
        out_specs=pl.BlockSpec((tm, D_MODEL), lambda i: (i, 0)),
        out_shape=jax.ShapeDtypeStruct((m, D_MODEL), F32),
        scratch_shapes=[pltpu.VMEM((D_MODEL, D_MODEL), BF16),
                        pltpu.VMEM((D_MODEL, D_MODEL), BF16),
                        pltpu.VMEM((PLE_DIM, D_MODEL), BF16)]
        + _stage_scratch(D_MODEL, STAGE_SLOTS),
        compiler_params=_params(1),
        name="out_ple",
    )(x2, m_act, w_o, ple_g, w_pg, p_l, w_pp, final_g)


def kernel(x, p, norm_mix_g, w_in, conv_w, lb_param, hg_norm_g, w_a_out, w_b_out, w_o,
           ple_norm_g, w_ple_gate, w_ple_proj, final_norm_g):
    batch, seq_len, d = x.shape
    depth = w_in.shape[0]
    m = batch * seq_len
    assert d == D_MODEL and seq_len % ROW_TILE == 0
    assert seq_len % MIX_TILE == 0 and MIX_TILE % CHUNK == 0
    assert w_in.shape[2] == 4 * D_CONV + 4 * D_REC + 2 * D_MODEL and D_CONV == D_REC

    mix_g = norm_mix_g.reshape(depth, 1, D_MODEL)
    ple_g = ple_norm_g.reshape(depth, 1, D_MODEL)
    hg_g = hg_norm_g.reshape(depth, 1, D_REC)
    final_g = final_norm_g.reshape(1, D_MODEL)
    p2 = p.reshape(depth, m, PLE_DIM)

    x2 = x.reshape(m, D_MODEL)
    for layer in range(depth):
        ya, yb = _mixers(x2, mix_g, w_in, conv_w, lb_param, hg_g, layer, seq_len)
        m_act = _merge(x2, mix_g, ya, yb, w_in, w_a_out, w_b_out, layer)
        x2 = _out_ple(x2, m_act, w_o, ple_g, w_ple_gate, p2, w_ple_proj, final_g, layer,
                      final=(layer == depth - 1))
    return x2.reshape(batch, seq_len, D_MODEL)
```

```python
import functools
import math

import numpy as np
import jax
import jax.numpy as jnp
from jax import lax
from jax.experimental import pallas as pl
from jax.experimental.pallas import tpu as pltpu

D_MODEL = 2048
D_CONV = 1024
D_REC = 1024
HG_HEADS = 8
HG_DIM = 128
PLE_DIM = 256
CONV_K = 3
EPS = 1e-6
LB_FLOOR = 1e-30

ROW_TILE = 512
MIX_TILE = 256
COL_BLOCK = 256
MERGE_BLOCK = 512
CHUNK = 128
N_LEVELS = int(math.log2(CHUNK))
SUBLANES = 8
HALO_ROWS = 16
STAGE_CHUNK_BYTES = 1024 * 1024
MIX_STAGE_SLOTS = 4
STAGE_SLOTS = 6
VMEM_LIMIT_BYTES = 58 * 1024 * 1024

F32 = jnp.float32
BF16 = jnp.bfloat16


def _dot(a, b):
    return jnp.dot(a, b, preferred_element_type=F32)


def _dot_tn(a, b):
    return lax.dot_general(a, b, (((0,), (0,)), ((), ())), preferred_element_type=F32)


def _rmsnorm(x, g):
    ms = jnp.mean(x * x, axis=-1, keepdims=True)
    return x * lax.rsqrt(ms + EPS) * g


def _sigmoid(x):
    return jax.nn.sigmoid(x)


def _silu(x):
    return x * jax.nn.sigmoid(x)


def _params(n_axes):
    return pltpu.CompilerParams(dimension_semantics=("arbitrary",) * n_axes,
                                vmem_limit_bytes=VMEM_LIMIT_BYTES)


def _stream_weight_bf16(w_hbm, layer, col0, dst_ref, stage_ref, sem):
    k, n = dst_ref.shape
    slots, rows = stage_ref.shape[0], min(stage_ref.shape[1], k)
    assert k % rows == 0
    n_chunks = k // rows
    ahead = slots - 1

    def chunk_copy(ci):
        slot = ci % slots
        r0 = pl.multiple_of(ci * rows, rows)
        return pltpu.make_async_copy(w_hbm.at[layer, pl.ds(r0, rows), pl.ds(col0, n)],
                                     stage_ref.at[slot, pl.ds(0, rows), pl.ds(0, n)],
                                     sem.at[slot])

    for ci in range(min(ahead, n_chunks)):
        chunk_copy(ci).start()

    def body(ci, carry):
        @pl.when(ci + ahead < n_chunks)
        def _():
            chunk_copy(ci + ahead).start()

        chunk_copy(ci).wait()
        r0 = pl.multiple_of(ci * rows, rows)
        dst_ref[pl.ds(r0, rows), :] = stage_ref[ci % slots, 0:rows, 0:n].astype(BF16)
        return carry

    lax.fori_loop(0, n_chunks, body, 0)


def _stage_scratch(width, slots):
    rows = STAGE_CHUNK_BYTES // (4 * width)
    return [pltpu.VMEM((slots, rows, width), F32), pltpu.SemaphoreType.DMA((slots,))]


def _level_table():
    c = CHUNK
    t = np.arange(c)[:, None]
    j = np.arange(c)[None, :]
    xor = np.maximum(t ^ j, 1)
    return np.where(j < t, np.floor(np.log2(xor)).astype(np.int32),
                    np.where(j == t, N_LEVELS, -1)).astype(np.int32)


def _bcast_row(ref, r, n):
    return jnp.broadcast_to(ref[r:r + 1, :], (n, HG_DIM))


def _level_operand(l, q, k, b, lf, b_ref, row):
    c = CHUNK
    h = 1 << l
    blk = 2 * h
    if l == 0:
        lower = (row & 1) != 0
        x = jnp.where(lower, lf, 0.0)
        m = jnp.where(lower, q, k)
    elif blk <= SUBLANES:
        lower = (row & h) != 0
        ref = None
        for sb in range(SUBLANES // blk):
            part = jnp.concatenate(
                [_bcast_row(b_ref, v0 + sb * blk + h - 1, SUBLANES)
                 for v0 in range(0, c, SUBLANES)], axis=0)
            ref = part if ref is None else jnp.where((row & (SUBLANES - 1)) < sb * blk, ref, part)
        x = jnp.where(lower, b - ref, ref - b)
        m = jnp.where(lower, q, k)
    else:
        xs, ms = [], []
        for b0 in range(0, c, blk):
            mid = _bcast_row(b_ref, b0 + h - 1, h)
            xs += [mid - b[b0:b0 + h], b[b0 + h:b0 + blk] - mid]
            ms += [k[b0:b0 + h], q[b0 + h:b0 + blk]]
        x = jnp.concatenate(xs, axis=0)
        m = jnp.concatenate(ms, axis=0)
    return (m * jnp.exp2(x)).astype(BF16)


OP_INTER = N_LEVELS
OP_STATE = N_LEVELS + 1
OP_DIAG = N_LEVELS + 2
N_OPERANDS = N_LEVELS + 3


def _mixers_kernel(x_ref, xh_ref, g_ref, w_in_hbm, cw_ref, lbp_ref, hg_ref, tril_ref, lvl_ref,
                   ya_ref, yb_ref,
                   wa_ref, wb_ref, stage_s, sem,
                   q_s, k_s, lf_s, b_s, v_s, og_s, bc_s, op_s, opt_s, stt_s, dec_s, state_s,
                   *, layer, tiles_per_seq):
    i = pl.program_id(0)
    seq_start = (i % tiles_per_seq) == 0

    @pl.when(i == 0)
    def _():
        _stream_weight_bf16(w_in_hbm, layer, 0, wa_ref, stage_s, sem)
        _stream_weight_bf16(w_in_hbm, layer, 4 * D_CONV, wb_ref, stage_s, sem)

    @pl.when(seq_start)
    def _():
        state_s[...] = jnp.zeros_like(state_s)

    lbp = lbp_ref[...]
    rows = [lbp[r:r + 1, :] for r in range(lbp.shape[0])]
    mx = functools.reduce(jnp.maximum, rows)
    ex = [jnp.exp(r - mx) for r in rows]
    den = functools.reduce(lambda a, b: a + b, ex)
    lb = jnp.zeros_like(mx)
    for r in range(1, layer + 1):
        lb = lb + ex[r] / den
    lb = jnp.clip(lb, 0.0, 1.0 - 1e-6)
    lb_floor = jnp.maximum(lb, LB_FLOOR)
    one_m_lb = 1.0 - lb

    c = CHUNK
    tm = x_ref.shape[0]
    n_chunks = tm // c
    g = g_ref[...]
    tril = tril_ref[...]
    h_ext = _rmsnorm(jnp.concatenate([xh_ref[...], x_ref[...]], axis=0), g).astype(BF16)
    h = h_ext[HALO_ROWS:]
    cb = COL_BLOCK
    heads_per_block = cb // HG_DIM

    for j in range(D_REC // cb):
        sl = slice(j * cb, (j + 1) * cb)
        u_q, u_f, u_i, u_g = [_dot(h, wb_ref[:, grp * D_REC + j * cb:grp * D_REC + (j + 1) * cb])
                              for grp in range(4)]
        qv = _silu(u_q)
        lfv = jnp.log2(lb_floor[:, sl] + one_m_lb[:, sl] * _sigmoid(u_f))
        kv = one_m_lb[:, sl] * _sigmoid(-u_f)
        ogv = _silu(u_g)
        for e in range(heads_per_block):
            head = j * heads_per_block + e
            es = slice(e * HG_DIM, (e + 1) * HG_DIM)
            q_s[head] = qv[:, es]
            k_s[head] = kv[:, es]
            lf_s[head] = lfv[:, es]
            v_s[head] = u_i[:, es].astype(BF16)
            og_s[head] = ogv[:, es]
        for ci in range(n_chunks):
            lfc = lfv[ci * c:(ci + 1) * c]
            hi = lfc.astype(BF16)
            rem = lfc - hi.astype(F32)
            mid = rem.astype(BF16)
            lo = (rem - mid.astype(F32)).astype(BF16)
            pre = _dot(tril, hi) + _dot(tril, mid) + _dot(tril, lo)
            for e in range(heads_per_block):
                b_s[j * heads_per_block + e, ci * c:(ci + 1) * c, :] = (
                    pre[:, e * HG_DIM:(e + 1) * HG_DIM])

    lvl = lvl_ref[...]
    row = lax.broadcasted_iota(jnp.int32, (c, HG_DIM), 0)
    row_a = lax.broadcasted_iota(jnp.int32, (tm, cb), 0)

    def prepare(ci, head):
        rs = slice(ci * c, (ci + 1) * c)
        q = q_s[head, rs, :]
        k = k_s[head, rs, :]
        lf = lf_s[head, rs, :]
        b = b_s[head, rs, :]
        bc = bc_s.at[head]
        bc[...] = b
        for l in range(N_LEVELS):
            op = _level_operand(l, q, k, b, lf, bc, row)
            op_s[ci, head, l] = op
            opt_s[ci, head, l] = op.T
        b_last = _bcast_row(bc, c - 1, c)
        op_s[ci, head, OP_INTER] = (q * jnp.exp2(b)).astype(BF16)
        op_s[ci, head, OP_STATE] = (k * jnp.exp2(b_last - b)).astype(BF16)
        diag = jnp.sum(q * k, axis=-1, keepdims=True)
        op_s[ci, head, OP_DIAG] = jnp.broadcast_to(diag, (c, HG_DIM)).astype(BF16)
        dec_s[ci, head] = jnp.exp2(b_last[0:SUBLANES, :])

    def attend_chunk(ci):
        att = [jnp.where(lvl == N_LEVELS, op_s[ci, hd, OP_DIAG], jnp.zeros((c, c), BF16))
               for hd in range(HG_HEADS)]
        for l in range(N_LEVELS):
            for hd in range(HG_HEADS):
                p_l = _dot(op_s[ci, hd, l], opt_s[ci, hd, l])
                att[hd] = jnp.where(lvl == l, p_l.astype(BF16), att[hd])
        return att

    def recur_chunk(ci, att, filler):
        rs = slice(ci * c, (ci + 1) * c)
        heads = range(HG_HEADS)
        states = [state_s[hd] for hd in heads]
        for hd in heads:
            stt_s[hd] = states[hd].astype(BF16).T
        inter = [_dot(op_s[ci, hd, OP_INTER], stt_s[hd]) for hd in heads]
        vals = [v_s[hd, rs, :] for hd in heads]
        for hd in heads:
            state_s[hd] = (states[hd] * dec_s[ci, hd][0:1, :]
                           + _dot_tn(vals[hd], op_s[ci, hd, OP_STATE]))
        outs = [inter[hd] + _dot(att[hd], vals[hd]) for hd in heads]
        filler()
        for hd in heads:
            o = outs[hd]
            o = o * lax.rsqrt(jnp.mean(o * o, axis=-1, keepdims=True) + EPS) * hg_ref[hd]
            yb_ref[rs, hd * HG_DIM:(hd + 1) * HG_DIM] = (o * og_s[hd, rs, :]).astype(yb_ref.dtype)

    def conv_block(j):
        w_c, w_b, w_x, w_z = [wa_ref[:, grp * D_CONV + j * cb:grp * D_CONV + (j + 1) * cb]
                              for grp in range(4)]
        v_ext = _dot(h_ext, w_c) * _dot(h_ext, w_x)
        v = v_ext[HALO_ROWS:]
        vh = jnp.where(seq_start, 0.0, v_ext[:HALO_ROWS])
        p1 = vh[HALO_ROWS - 1:HALO_ROWS, :]
        p2 = vh[HALO_ROWS - 2:HALO_ROWS - 1, :]
        v1 = jnp.where(row_a == 0, p1, pltpu.roll(v, 1, 0))
        v2 = jnp.where(row_a == 0, p2, jnp.where(row_a == 1, p1, pltpu.roll(v, 2, 0)))
        cw = cw_ref[:, j * cb:(j + 1) * cb]
        conv = v2 * cw[0:1, :] + v1 * cw[1:2, :] + v * cw[2:3, :]
        ya = _dot(h, w_b) * conv * _silu(_dot(h, w_z))
        ya_ref[:, j * cb:(j + 1) * cb] = ya.astype(ya_ref.dtype)

    n_conv = D_CONV // cb
    n_late = min(n_chunks, n_conv)
    for ci in range(n_chunks):
        for hd in range(HG_HEADS):
            prepare(ci, hd)
    for j in range(n_conv - n_late):
        conv_block(j)
    atts = [attend_chunk(ci) for ci in range(n_chunks)]
    for ci in range(n_chunks):
        late = n_conv - n_chunks + ci
        recur_chunk(ci, atts[ci], (lambda j=late: conv_block(j)) if late >= 0 else (lambda: None))


def _mixers(x2, g, w_in, conv_w, lb_param, hg_g, layer, seq_len):
    m = x2.shape[0]
    tm = MIX_TILE
    c = CHUNK
    n_chunks = tm // c
    tril = jnp.asarray(np.tril(np.ones((c, c), np.float32)), BF16)
    level = jnp.asarray(_level_table(), BF16)
    depth = lb_param.shape[0]
    hg_heads = hg_g.reshape(depth, HG_HEADS, 1, HG_DIM)
    halo_blocks_per_tile = tm // HALO_ROWS
    kern = functools.partial(_mixers_kernel, layer=layer, tiles_per_seq=seq_len // tm)
    const2 = lambda i: (0, 0)
    head_f32 = pltpu.VMEM((HG_HEADS, tm, HG_DIM), F32)
    head_bf16 = pltpu.VMEM((HG_HEADS, tm, HG_DIM), BF16)
    return pl.pallas_call(
        kern,
        grid=(m // tm,),
        in_specs=[
            pl.BlockSpec((tm, D_MODEL), lambda i: (i, 0)),
            pl.BlockSpec((HALO_ROWS, D_MODEL),
                         lambda i: (jnp.maximum(i * halo_blocks_per_tile - 1, 0), 0)),
            pl.BlockSpec((None, 1, D_MODEL), lambda i: (layer, 0, 0)),
            pl.BlockSpec(memory_space=pl.ANY),
            pl.BlockSpec((None, CONV_K, D_CONV), lambda i: (layer, 0, 0)),
            pl.BlockSpec((depth, D_REC), const2),
            pl.BlockSpec((None, HG_HEADS, 1, HG_DIM), lambda i: (layer, 0, 0, 0)),
            pl.BlockSpec(tril.shape, const2),
            pl.BlockSpec(level.shape, const2),
        ],
        out_specs=[pl.BlockSpec((tm, D_CONV), lambda i: (i, 0)),
                   pl.BlockSpec((tm, D_REC), lambda i: (i, 0))],
        out_shape=[jax.ShapeDtypeStruct((m, D_CONV), BF16),
                   jax.ShapeDtypeStruct((m, D_REC), BF16)],
        scratch_shapes=[pltpu.VMEM((D_MODEL, 4 * D_CONV), BF16),
                        pltpu.VMEM((D_MODEL, 4 * D_REC), BF16)]
        + _stage_scratch(4 * D_CONV, MIX_STAGE_SLOTS)
        + [head_f32, head_f32, head_f32, head_f32, head_bf16, head_f32,
           pltpu.VMEM((HG_HEADS, c, HG_DIM), F32),
           pltpu.VMEM((n_chunks, HG_HEADS, N_OPERANDS, c, HG_DIM), BF16),
           pltpu.VMEM((n_chunks, HG_HEADS, N_LEVELS, HG_DIM, c), BF16),
           pltpu.VMEM((HG_HEADS, HG_DIM, HG_DIM), BF16),
           pltpu.VMEM((n_chunks, HG_HEADS, SUBLANES, HG_DIM), F32),
           pltpu.VMEM((HG_HEADS, HG_DIM, HG_DIM), F32)],
        compiler_params=_params(1),
        name="mixers",
    )(x2, x2, g, w_in, conv_w, lb_param, hg_heads, tril, level)


def _merge_kernel(x_ref, g_ref, ya_ref, yb_ref, w_in_hbm, wa_hbm, wb_hbm, m_ref,
                  wg_ref, wa_ref, wb_ref, stage_s, sem, *, layer):
    @pl.when(pl.program_id(0) == 0)
    def _():
        _stream_weight_bf16(w_in_hbm, layer, 4 * D_CONV + 4 * D_REC, wg_ref, stage_s, sem)
        _stream_weight_bf16(wa_hbm, layer, 0, wa_ref, stage_s, sem)
        _stream_weight_bf16(wb_hbm, layer, 0, wb_ref, stage_s, sem)

    h = _rmsnorm(x_ref[...], g_ref[...]).astype(BF16)
    ya = ya_ref[...]
    yb = yb_ref[...]
    nb = MERGE_BLOCK
    for n in range(D_MODEL // nb):
        sl = slice(n * nb, (n + 1) * nb)
        gate_a = _dot(h, wg_ref[:, n * nb:(n + 1) * nb])
        gate_b = _dot(h, wg_ref[:, D_MODEL + n * nb:D_MODEL + (n + 1) * nb])
        m = (_sigmoid(gate_a) * _dot(ya, wa_ref[:, sl])
             + _sigmoid(gate_b) * _dot(yb, wb_ref[:, sl]))
        m_ref[:, sl] = m.astype(m_ref.dtype)


def _merge(x2, g, ya, yb, w_in, w_a_out, w_b_out, layer):
    m = x2.shape[0]
    tm = ROW_TILE
    return pl.pallas_call(
        functools.partial(_merge_kernel, layer=layer),
        grid=(m // tm,),
        in_specs=[
            pl.BlockSpec((tm, D_MODEL), lambda i: (i, 0)),
            pl.BlockSpec((None, 1, D_MODEL), lambda i: (layer, 0, 0)),
            pl.BlockSpec((tm, D_CONV), lambda i: (i, 0)),
            pl.BlockSpec((tm, D_REC), lambda i: (i, 0)),
            pl.BlockSpec(memory_space=pl.ANY),
            pl.BlockSpec(memory_space=pl.ANY),
            pl.BlockSpec(memory_space=pl.ANY),
        ],
        out_specs=pl.BlockSpec((tm, D_MODEL), lambda i: (i, 0)),
        out_shape=jax.ShapeDtypeStruct((m, D_MODEL), BF16),
        scratch_shapes=[pltpu.VMEM((D_MODEL, 2 * D_MODEL), BF16),
                        pltpu.VMEM((D_CONV, D_MODEL), BF16),
                        pltpu.VMEM((D_REC, D_MODEL), BF16)]
        + _stage_scratch(2 * D_MODEL, STAGE_SLOTS),
        compiler_params=_params(1),
        name="merge",
    )(x2, g, ya, yb, w_in, w_a_out, w_b_out)


def _out_ple_kernel(x_ref, m_ref, wo_hbm, pg_ref, wpg_hbm, p_ref, wpp_hbm, fg_ref, o_ref,
                    wo_ref, wpg_ref, wpp_ref, stage_s, sem, *, layer, final):
    @pl.when(pl.program_id(0) == 0)
    def _():
        _stream_weight_bf16(wo_hbm, layer, 0, wo_ref, stage_s, sem)
        _stream_weight_bf16(wpg_hbm, layer, 0, wpg_ref, stage_s, sem)
        _stream_weight_bf16(wpp_hbm, layer, 0, wpp_ref, stage_s, sem)

    o_ref[...] = x_ref[...] + _dot(m_ref[...], wo_ref[...])
    xn = _rmsnorm(o_ref[...], pg_ref[...]).astype(BF16)
    p = p_ref[...].astype(BF16)
    nb = MERGE_BLOCK
    for n in range(D_MODEL // nb):
        sl = slice(n * nb, (n + 1) * nb)
        gate = _sigmoid(_dot(xn, wpg_ref[:, sl]))
        o_ref[:, sl] = o_ref[:, sl] + gate * _dot(p, wpp_ref[:, sl])
    if final:
        o_ref[...] = _rmsnorm(o_ref[...], fg_ref[...])


def _out_ple(x2, m_act, w_o, ple_g, w_pg, p_l, w_pp, final_g, layer, final):
    m = x2.shape[0]
    tm = ROW_TILE
    kern = functools.partial(_out_ple_kernel, layer=layer, final=final)
    return pl.pallas_call(
        kern,
        grid=(m // tm,),
        in_specs=[
            pl.BlockSpec((tm, D_MODEL), lambda i: (i, 0)),
            pl.BlockSpec((tm, D_MODEL), lambda i: (i, 0)),
            pl.BlockSpec(memory_space=pl.ANY),
            pl.BlockSpec((None, 1, D_MODEL), lambda i: (layer, 0, 0)),
            pl.BlockSpec(memory_space=pl.ANY),
            pl.BlockSpec((None, tm, PLE_DIM), lambda i: (layer, i, 0)),
            pl.BlockSpec(memory_space=pl.ANY),
            pl.BlockSpec((1, D_MODEL), lambda i: (0, 0)),
        ],
        out_specs=pl.BlockSpec((tm, D_MODEL), lambda i: (i, 0)),
        out_shape=jax.ShapeDtypeStruct((m, D_MODEL), F32),
        scratch_shapes=[pltpu.VMEM((D_MODEL, D_MODEL), BF16),
                        pltpu.VMEM((D_MODEL, D_MODEL), BF16),
                        pltpu.VMEM((PLE_DIM, D_MODEL), BF16)]
        + _stage_scratch(D_MODEL, STAGE_SLOTS),
        compiler_params=_params(1),
        name="out_ple",
    )(x2, m_act, w_o, ple_g, w_pg, p_l, w_pp, final_g)


def kernel(x, p, norm_mix_g, w_in, conv_w, lb_param, hg_norm_g, w_a_out, w_b_out, w_o,
           ple_norm_g, w_ple_gate, w_ple_proj, final_norm_g):
    batch, seq_len, d = x.shape
    depth = w_in.shape[0]
    m = batch * seq_len
    assert d == D_MODEL and seq_len % ROW_TILE == 0
    assert seq_len % MIX_TILE == 0 and MIX_TILE % CHUNK == 0
    assert w_in.shape[2] == 4 * D_CONV + 4 * D_REC + 2 * D_MODEL and D_CONV == D_REC

    mix_g = norm_mix_g.reshape(depth, 1, D_MODEL)
    ple_g = ple_norm_g.reshape(depth, 1, D_MODEL)
    hg_g = hg_norm_g.reshape(depth, 1, D_REC)
    final_g = final_norm_g.reshape(1, D_MODEL)
    p2 = p.reshape(depth, m, PLE_DIM)

    x2 = x.reshape(m, D_MODEL)
    for layer in range(depth):
        ya, yb = _mixers(x2, mix_g, w_in, conv_w, lb_param, hg_g, layer, seq_len)
        m_act = _merge(x2, mix_g, ya, yb, w_in, w_a_out, w_b_out, layer)
        x2 = _out_ple(x2, m_act, w_o, ple_g, w_ple_gate, p2, w_ple_proj, final_g, layer,
                      final=(layer == depth - 1))
    return x2.reshape(batch, seq_len, D_MODEL)
```

```python
import functools
import math

import numpy as np
import jax
import jax.numpy as jnp
from jax import lax
from jax.experimental import pallas as pl
from jax.experimental.pallas import tpu as pltpu

D_MODEL = 2048
D_CONV = 1024
D_REC = 1024
HG_HEADS = 8
HG_DIM = 128
PLE_DIM = 256
CONV_K = 3
EPS = 1e-6
LB_FLOOR = 1e-30

ROW_TILE = 512
MIX_TILE = 256
COL_BLOCK = 256
MERGE_BLOCK = 512
CHUNK = 128
N_LEVELS = int(math.log2(CHUNK))
SUBLANES = 8
HALO_ROWS = 16
STAGE_CHUNK_BYTES = 1024 * 1024
MIX_STAGE_SLOTS = 4
STAGE_SLOTS = 6
VMEM_LIMIT_BYTES = 58 * 1024 * 1024

F32 = jnp.float32
BF16 = jnp.bfloat16


def _dot(a, b):
    return jnp.dot(a, b, preferred_element_type=F32)


def _dot_tn(a, b):
    return lax.dot_general(a, b, (((0,), (0,)), ((), ())), preferred_element_type=F32)


def _rmsnorm(x, g):
    ms = jnp.mean(x * x, axis=-1, keepdims=True)
    return x * lax.rsqrt(ms + EPS) * g


def _sigmoid(x):
    return jax.nn.sigmoid(x)


def _silu(x):
    return x * jax.nn.sigmoid(x)


def _params(n_axes):
    return pltpu.CompilerParams(dimension_semantics=("arbitrary",) * n_axes,
                                vmem_limit_bytes=VMEM_LIMIT_BYTES)


def _stream_weight_bf16(w_hbm, layer, col0, dst_ref, stage_ref, sem):
    k, n = dst_ref.shape
    slots, rows = stage_ref.shape[0], min(stage_ref.shape[1], k)
    assert k % rows == 0
    n_chunks = k // rows
    ahead = slots - 1

    def chunk_copy(ci):
        slot = ci % slots
        r0 = pl.multiple_of(ci * rows, rows)
        return pltpu.make_async_copy(w_hbm.at[layer, pl.ds(r0, rows), pl.ds(col0, n)],
                                     stage_ref.at[slot, pl.ds(0, rows), pl.ds(0, n)],
                                     sem.at[slot])

    for ci in range(min(ahead, n_chunks)):
        chunk_copy(ci).start()

    def body(ci, carry):
        @pl.when(ci + ahead < n_chunks)
        def _():
            chunk_copy(ci + ahead).start()

        chunk_copy(ci).wait()
        r0 = pl.multiple_of(ci * rows, rows)
        dst_ref[pl.ds(r0, rows), :] = stage_ref[ci % slots, 0:rows, 0:n].astype(BF16)
        return carry

    lax.fori_loop(0, n_chunks, body, 0)


def _stage_scratch(width, slots):
    rows = STAGE_CHUNK_BYTES // (4 * width)
    return [pltpu.VMEM((slots, rows, width), F32), pltpu.SemaphoreType.DMA((slots,))]


def _level_table():
    c = CHUNK
    t = np.arange(c)[:, None]
    j = np.arange(c)[None, :]
    xor = np.maximum(t ^ j, 1)
    return np.where(j < t, np.floor(np.log2(xor)).astype(np.int32),
                    np.where(j == t, N_LEVELS, -1)).astype(np.int32)


def _bcast_row(ref, r, n):
    return jnp.broadcast_to(ref[r:r + 1, :], (n, HG_DIM))


def _level_operand(l, q, k, b, lf, b_ref, row):
    c = CHUNK
    h = 1 << l
    blk = 2 * h
    if l == 0:
        lower = (row & 1) != 0
        x = jnp.where(lower, lf, 0.0)
        m = jnp.where(lower, q, k)
    elif blk <= SUBLANES:
        lower = (row & h) != 0
        ref = None
        for sb in range(SUBLANES // blk):
            part = jnp.concatenate(
                [_bcast_row(b_ref, v0 + sb * blk + h - 1, SUBLANES)
                 for v0 in range(0, c, SUBLANES)], axis=0)
            ref = part if ref is None else jnp.where((row & (SUBLANES - 1)) < sb * blk, ref, part)
        x = jnp.where(lower, b - ref, ref - b)
        m = jnp.where(lower, q, k)
    else:
        xs, ms = [], []
        for b0 in range(0, c, blk):
            mid = _bcast_row(b_ref, b0 + h - 1, h)
            xs += [mid - b[b0:b0 + h], b[b0 + h:b0 + blk] - mid]
            ms += [k[b0:b0 + h], q[b0 + h:b0 + blk]]
        x = jnp.concatenate(xs, axis=0)
        m = jnp.concatenate(ms, axis=0)
    return (m * jnp.exp2(x)).astype(BF16)


OP_INTER = N_LEVELS
OP_STATE = N_LEVELS + 1
OP_DIAG = N_LEVELS + 2
N_OPERANDS = N_LEVELS + 3


def _mixers_kernel(x_ref, xh_ref, g_ref, w_in_hbm, cw_ref, lbp_ref, hg_ref, tril_ref, lvl_ref,
                   ya_ref, yb_ref,
                   wa_ref, wb_ref, stage_s, sem,
                   q_s, k_s, lf_s, b_s, v_s, og_s, bc_s, op_s, opt_s, stt_s, dec_s, state_s,
                   *, layer, tiles_per_seq):
    i = pl.program_id(0)
    seq_start = (i % tiles_per_seq) == 0

    @pl.when(i == 0)
    def _():
        _stream_weight_bf16(w_in_hbm, layer, 0, wa_ref, stage_s, sem)
        _stream_weight_bf16(w_in_hbm, layer, 4 * D_CONV, wb_ref, stage_s, sem)

    @pl.when(seq_start)
    def _():
        state_s[...] = jnp.zeros_like(state_s)

    lbp = lbp_ref[...]
    rows = [lbp[r:r + 1, :] for r in range(lbp.shape[0])]
    mx = functools.reduce(jnp.maximum, rows)
    ex = [jnp.exp(r - mx) for r in rows]
    den = functools.reduce(lambda a, b: a + b, ex)
    lb = jnp.zeros_like(mx)
    for r in range(1, layer + 1):
        lb = lb + ex[r] / den
    lb = jnp.clip(lb, 0.0, 1.0 - 1e-6)
    lb_floor = jnp.maximum(lb, LB_FLOOR)
    one_m_lb = 1.0 - lb

    c = CHUNK
    tm = x_ref.shape[0]
    n_chunks = tm // c
    g = g_ref[...]
    tril = tril_ref[...]
    h_ext = _rmsnorm(jnp.concatenate([xh_ref[...], x_ref[...]], axis=0), g).astype(BF16)
    h = h_ext[HALO_ROWS:]
    cb = COL_BLOCK
    heads_per_block = cb // HG_DIM

    for j in range(D_REC // cb):
        sl = slice(j * cb, (j + 1) * cb)
        u_q, u_f, u_i, u_g = [_dot(h, wb_ref[:, grp * D_REC + j * cb:grp * D_REC + (j + 1) * cb])
                              for grp in range(4)]
        qv = _silu(u_q)
        lfv = jnp.log2(lb_floor[:, sl] + one_m_lb[:, sl] * _sigmoid(u_f))
        kv = one_m_lb[:, sl] * _sigmoid(-u_f)
        ogv = _silu(u_g)
        for e in range(heads_per_block):
            head = j * heads_per_block + e
            es = slice(e * HG_DIM, (e + 1) * HG_DIM)
            q_s[head] = qv[:, es]
            k_s[head] = kv[:, es]
            lf_s[head] = lfv[:, es]
            v_s[head] = u_i[:, es].astype(BF16)
            og_s[head] = ogv[:, es]
        for ci in range(n_chunks):
            lfc = lfv[ci * c:(ci + 1) * c]
            hi = lfc.astype(BF16)
            lo = (lfc - hi.astype(F32)).astype(BF16)
            pre = _dot(tril, hi) + _dot(tril, lo)
            for e in range(heads_per_block):
                b_s[j * heads_per_block + e, ci * c:(ci + 1) * c, :] = (
                    pre[:, e * HG_DIM:(e + 1) * HG_DIM])

    lvl = lvl_ref[...]
    row = lax.broadcasted_iota(jnp.int32, (c, HG_DIM), 0)
    row_a = lax.broadcasted_iota(jnp.int32, (tm, cb), 0)

    def prepare(ci, head):
        rs = slice(ci * c, (ci + 1) * c)
        q = q_s[head, rs, :]
        k = k_s[head, rs, :]
        lf = lf_s[head, rs, :]
        b = b_s[head, rs, :]
        bc = bc_s.at[head]
        bc[...] = b
        for l in range(N_LEVELS):
            op = _level_operand(l, q, k, b, lf, bc, row)
            op_s[ci, head, l] = op
            opt_s[ci, head, l] = op.T
        b_last = _bcast_row(bc, c - 1, c)
        op_s[ci, head, OP_INTER] = (q * jnp.exp2(b)).astype(BF16)
        op_s[ci, head, OP_STATE] = (k * jnp.exp2(b_last - b)).astype(BF16)
        diag = jnp.sum(q * k, axis=-1, keepdims=True)
        op_s[ci, head, OP_DIAG] = jnp.broadcast_to(diag, (c, HG_DIM)).astype(BF16)
        dec_s[ci, head] = jnp.exp2(b_last[0:SUBLANES, :])

    def attend_chunk(ci):
        att = [jnp.where(lvl == N_LEVELS, op_s[ci, hd, OP_DIAG], jnp.zeros((c, c), BF16))
               for hd in range(HG_HEADS)]
        for l in range(N_LEVELS):
            for hd in range(HG_HEADS):
                p_l = _dot(op_s[ci, hd, l], opt_s[ci, hd, l])
                att[hd] = jnp.where(lvl == l, p_l.astype(BF16), att[hd])
        return att

    def recur_chunk(ci, att, filler):
        rs = slice(ci * c, (ci + 1) * c)
        heads = range(HG_HEADS)
        states = [state_s[hd] for hd in heads]
        for hd in heads:
            stt_s[hd] = states[hd].astype(BF16).T
        inter = [_dot(op_s[ci, hd, OP_INTER], stt_s[hd]) for hd in heads]
        vals = [v_s[hd, rs, :] for hd in heads]
        for hd in heads:
            state_s[hd] = (states[hd] * dec_s[ci, hd][0:1, :]
                           + _dot_tn(vals[hd], op_s[ci, hd, OP_STATE]))
        outs = [inter[hd] + _dot(att[hd], vals[hd]) for hd in heads]
        filler()
        for hd in heads:
            o = outs[hd]
            o = o * lax.rsqrt(jnp.mean(o * o, axis=-1, keepdims=True) + EPS) * hg_ref[hd]
            yb_ref[rs, hd * HG_DIM:(hd + 1) * HG_DIM] = (o * og_s[hd, rs, :]).astype(yb_ref.dtype)

    def conv_block(j):
        w_c, w_b, w_x, w_z = [wa_ref[:, grp * D_CONV + j * cb:grp * D_CONV + (j + 1) * cb]
                              for grp in range(4)]
        v_ext = _dot(h_ext, w_c) * _dot(h_ext, w_x)
        v = v_ext[HALO_ROWS:]
        vh = jnp.where(seq_start, 0.0, v_ext[:HALO_ROWS])
        p1 = vh[HALO_ROWS - 1:HALO_ROWS, :]
        p2 = vh[HALO_ROWS - 2:HALO_ROWS - 1, :]
        v1 = jnp.where(row_a == 0, p1, pltpu.roll(v, 1, 0))
        v2 = jnp.where(row_a == 0, p2, jnp.where(row_a == 1, p1, pltpu.roll(v, 2, 0)))
        cw = cw_ref[:, j * cb:(j + 1) * cb]
        conv = v2 * cw[0:1, :] + v1 * cw[1:2, :] + v * cw[2:3, :]
        ya = _dot(h, w_b) * conv * _silu(_dot(h, w_z))
        ya_ref[:, j * cb:(j + 1) * cb] = ya.astype(ya_ref.dtype)

    n_conv = D_CONV // cb
    n_late = min(n_chunks, n_conv)
    for ci in range(n_chunks):
        for hd in range(HG_HEADS):
            prepare(ci, hd)
    for j in range(n_conv - n_late):
        conv_block(j)
    atts = [attend_chunk(ci) for ci in range(n_chunks)]
    for ci in range(n_chunks):
        late = n_conv - n_chunks + ci
        recur_chunk(ci, atts[ci], (lambda j=late: conv_block(j)) if late >= 0 else (lambda: None))


def _mixers(x2, g, w_in, conv_w, lb_param, hg_g, layer, seq_len):
    m = x2.shape[0]
    tm = MIX_TILE
    c = CHUNK
    n_chunks = tm // c
    tril = jnp.asarray(np.tril(np.ones((c, c), np.float32)), BF16)
    level = jnp.asarray(_level_table(), BF16)
    depth = lb_param.shape[0]
    hg_heads = hg_g.reshape(depth, HG_HEADS, 1, HG_DIM)
    halo_blocks_per_tile = tm // HALO_ROWS
    kern = functools.partial(_mixers_kernel, layer=layer, tiles_per_seq=seq_len // tm)
    const2 = lambda i: (0, 0)
    head_f32 = pltpu.VMEM((HG_HEADS, tm, HG_DIM), F32)
    head_bf16 = pltpu.VMEM((HG_HEADS, tm, HG_DIM), BF16)
    return pl.pallas_call(
        kern,
        grid=(m // tm,),
        in_specs=[
            pl.BlockSpec((tm, D_MODEL), lambda i: (i, 0)),
            pl.BlockSpec((HALO_ROWS, D_MODEL),
                         lambda i: (jnp.maximum(i * halo_blocks_per_tile - 1, 0), 0)),
            pl.BlockSpec((None, 1, D_MODEL), lambda i: (layer, 0, 0)),
            pl.BlockSpec(memory_space=pl.ANY),
            pl.BlockSpec((None, CONV_K, D_CONV), lambda i: (layer, 0, 0)),
            pl.BlockSpec((depth, D_REC), const2),
            pl.BlockSpec((None, HG_HEADS, 1, HG_DIM), lambda i: (layer, 0, 0, 0)),
            pl.BlockSpec(tril.shape, const2),
            pl.BlockSpec(level.shape, const2),
        ],
        out_specs=[pl.BlockSpec((tm, D_CONV), lambda i: (i, 0)),
                   pl.BlockSpec((tm, D_REC), lambda i: (i, 0))],
        out_shape=[jax.ShapeDtypeStruct((m, D_CONV), BF16),
                   jax.ShapeDtypeStruct((m, D_REC), BF16)],
        scratch_shapes=[pltpu.VMEM((D_MODEL, 4 * D_CONV), BF16),
                        pltpu.VMEM((D_MODEL, 4 * D_REC), BF16)]
        + _stage_scratch(4 * D_CONV, MIX_STAGE_SLOTS)
        + [head_f32, head_f32, head_f32, head_f32, head_bf16, head_f32,
           pltpu.VMEM((HG_HEADS, c, HG_DIM), F32),
           pltpu.VMEM((n_chunks, HG_HEADS, N_OPERANDS, c, HG_DIM), BF16),
           pltpu.VMEM((n_chunks, HG_HEADS, N_LEVELS, HG_DIM, c), BF16),
           pltpu.VMEM((HG_HEADS, HG_DIM, HG_DIM), BF16),
           pltpu.VMEM((n_chunks, HG_HEADS, SUBLANES, HG_DIM), F32),
           pltpu.VMEM((HG_HEADS, HG_DIM, HG_DIM), F32)],
        compiler_params=_params(1),
        name="mixers",
    )(x2, x2, g, w_in, conv_w, lb_param, hg_heads, tril, level)


def _merge_kernel(x_ref, g_ref, ya_ref, yb_ref, w_in_hbm, wa_hbm, wb_hbm, m_ref,
                  wg_ref, wa_ref, wb_ref, stage_s, sem, *, layer):
    @pl.when(pl.program_id(0) == 0)
    def _():
        _stream_weight_bf16(w_in_hbm, layer, 4 * D_CONV + 4 * D_REC, wg_ref, stage_s, sem)
        _stream_weight_bf16(wa_hbm, layer, 0, wa_ref, stage_s, sem)
        _stream_weight_bf16(wb_hbm, layer, 0, wb_ref, stage_s, sem)

    h = _rmsnorm(x_ref[...], g_ref[...]).astype(BF16)
    ya = ya_ref[...]
    yb = yb_ref[...]
    nb = MERGE_BLOCK
    for n in range(D_MODEL // nb):
        sl = slice(n * nb, (n + 1) * nb)
        gate_a = _dot(h, wg_ref[:, n * nb:(n + 1) * nb])
        gate_b = _dot(h, wg_ref[:, D_MODEL + n * nb:D_MODEL + (n + 1) * nb])
        m = (_sigmoid(gate_a) * _dot(ya, wa_ref[:, sl])
             + _sigmoid(gate_b) * _dot(yb, wb_ref[:, sl]))
        m_ref[:, sl] = m.astype(m_ref.dtype)


def _merge(x2, g, ya, yb, w_in, w_a_out, w_b_out, layer):
    m = x2.shape[0]
    tm = ROW_TILE
    return pl.pallas_call(
        functools.partial(_merge_kernel, layer=layer),
        grid=(m // tm,),
        in_specs=[
            pl.BlockSpec((tm, D_MODEL), lambda i: (i, 0)),
            pl.BlockSpec((None, 1, D_MODEL), lambda i: (layer, 0, 0)),
            pl.BlockSpec((tm, D_CONV), lambda i: (i, 0)),
            pl.BlockSpec((tm, D_REC), lambda i: (i, 0)),
            pl.BlockSpec(memory_space=pl.ANY),
            pl.BlockSpec(memory_space=pl.ANY),
            pl.BlockSpec(memory_space=pl.ANY),
        ],
        out_specs=pl.BlockSpec((tm, D_MODEL), lambda i: (i, 0)),
        out_shape=jax.ShapeDtypeStruct((m, D_MODEL), BF16),
        scratch_shapes=[pltpu.VMEM((D_MODEL, 2 * D_MODEL), BF16),
                        pltpu.VMEM((D_CONV, D_MODEL), BF16),
                        pltpu.VMEM((D_REC, D_MODEL), BF16)]
        + _stage_scratch(2 * D_MODEL, STAGE_SLOTS),
        compiler_params=_params(1),
        name="merge",
    )(x2, g, ya, yb, w_in, w_a_out, w_b_out)


def _out_ple_kernel(x_ref, m_ref, wo_hbm, pg_ref, wpg_hbm, p_ref, wpp_hbm, fg_ref, o_ref,
                    wo_ref, wpg_ref, wpp_ref, stage_s, sem, *, layer, final):
    @pl.when(pl.program_id(0) == 0)
    def _():
        _stream_weight_bf16(wo_hbm, layer, 0, wo_ref, stage_s, sem)
        _stream_weight_bf16(wpg_hbm, layer, 0, wpg_ref, stage_s, sem)
        _stream_weight_bf16(wpp_hbm, layer, 0, wpp_ref, stage_s, sem)

    o_ref[...] = x_ref[...] + _dot(m_ref[...], wo_ref[...])
    xn = _rmsnorm(o_ref[...], pg_ref[...]).astype(BF16)
    p = p_ref[...].astype(BF16)
    nb = MERGE_BLOCK
    for n in range(D_MODEL // nb):
        sl = slice(n * nb, (n + 1) * nb)
        gate = _sigmoid(_dot(xn, wpg_ref[:, sl]))
        o_ref[:, sl] = o_ref[:, sl] + gate * _dot(p, wpp_ref[:, sl])
    if final:
        o_ref[...] = _rmsnorm(o_ref[...], fg_ref[...])


def _out_ple(x2, m_act, w_o, ple_g, w_pg, p_l, w_pp, final_g, layer, final):
    m = x2.shape[0]
    tm = ROW_TILE
    kern = functools.partial(_out_ple_kernel, layer=layer, final=final)
    return pl.pallas_call(
        kern,
        grid=(m // tm,),
        in_specs=[
            pl.BlockSpec((tm, D_MODEL), lambda i: (i, 0)),
            pl.BlockSpec((tm, D_MODEL), lambda i: (i, 0)),
            pl.BlockSpec(memory_space=pl.ANY),
            pl.BlockSpec((None, 1, D_MODEL), lambda i: (layer, 0, 0)),
            pl.BlockSpec(memory_space=pl.ANY),
            pl.BlockSpec((None, tm, PLE_DIM), lambda i: (layer, i, 0)),
            pl.BlockSpec(memory_space=pl.ANY),
            pl.BlockSpec((1, D_MODEL), lambda i: (0, 0)),
        ],
        out_specs=pl.BlockSpec((tm, D_MODEL), lambda i: (i, 0)),
        out_shape=jax.ShapeDtypeStruct((m, D_MODEL), F32),
        scratch_shapes=[pltpu.VMEM((D_MODEL, D_MODEL), BF16),
                        pltpu.VMEM((D_MODEL, D_MODEL), BF16),
                        pltpu.VMEM((PLE_DIM, D_MODEL), BF16)]
        + _stage_scratch(D_MODEL, STAGE_SLOTS),
        compiler_params=_params(1),
        name="out_ple",
    )(x2, m_act, w_o, ple_g, w_pg, p_l, w_pp, final_g)


def kernel(x, p, norm_mix_g, w_in, conv_w, lb_param, hg_norm_g, w_a_out, w_b_out, w_o,
           ple_norm_g, w_ple_gate, w_ple_proj, final_norm_g):
    batch, seq_len, d = x.shape
    depth = w_in.shape[0]
    m = batch * seq_len
    assert d == D_MODEL and seq_len % ROW_TILE == 0
    assert seq_len % MIX_TILE == 0 and MIX_TILE % CHUNK == 0
    assert w_in.shape[2] == 4 * D_CONV + 4 * D_REC + 2 * D_MODEL and D_CONV == D_REC

    mix_g = norm_mix_g.reshape(depth, 1, D_MODEL)
    ple_g = ple_norm_g.reshape(depth, 1, D_MODEL)
    hg_g = hg_norm_g.reshape(depth, 1, D_REC)
    final_g = final_norm_g.reshape(1, D_MODEL)
    p2 = p.reshape(depth, m, PLE_DIM)

    x2 = x.reshape(m, D_MODEL)
    for layer in range(depth):
        ya, yb = _mixers(x2, mix_g, w_in, conv_w, lb_param, hg_g, layer, seq_len)
        m_act = _merge(x2, mix_g, ya, yb, w_in, w_a_out, w_b_out, layer)
        x2 = _out_ple(x2, m_act, w_o, ple_g, w_ple_gate, p2, w_ple_proj, final_g, layer,
                      final=(layer == depth - 1))
    return x2.reshape(batch, seq_len, D_MODEL)
```
